```python
import jax, jax.numpy as jnp
from jax import lax
import numpy as np

D_MODEL = 1024
BATCH = 8
SEQ = 4096
DEPTH = 1

GRID_W = 64
CTX_LEN = 256
HGRN_HEADS = 8
HGRN_EXPAND = 128
HGRN_DIM = HGRN_HEADS * HGRN_EXPAND
HGRN_HEAD_V = HGRN_DIM // HGRN_HEADS
CONV_DIM = D_MODEL
CONV_WIDTH = 31
FFN_DIM = 2816
N_MOD = 9
IN_COLS = 5 * HGRN_DIM + 2 * CONV_DIM
EPS = 1e-6

kernel_name = "hybrid_hgrn2_conformer_dit_block"


def rms_norm(x, gain):
    x32 = x.astype(jnp.float32)
    y = x32 * lax.rsqrt(jnp.mean(x32 * x32, axis=-1, keepdims=True) + EPS)
    return y.astype(x.dtype) * gain


def layer_norm(x, gain, bias):
    x32 = x.astype(jnp.float32)
    mu = jnp.mean(x32, axis=-1, keepdims=True)
    var = jnp.mean(jnp.square(x32 - mu), axis=-1, keepdims=True)
    y = (x32 - mu) * lax.rsqrt(var + EPS)
    return y.astype(x.dtype) * gain + bias


def modulate(x, shift, scale):
    return x * (1.0 + scale) + shift


def swiglu(u, w_up, w_down):
    a, b = jnp.split(u @ w_up, 2, axis=-1)
    return (jax.nn.silu(a) * b) @ w_down


def ffn_sublayer(h, gain, w_up, w_down, shift, scale, gate):
    return h + 0.5 * gate * swiglu(modulate(rms_norm(h, gain), shift, scale), w_up, w_down)


def to_heads(t):
    b, n, _ = t.shape
    return t.reshape(b, n, HGRN_HEADS, -1).transpose(0, 2, 1, 3)


def chunk_scan(q, f, v, s0):
    b, nh, n, dk = q.shape
    dv = v.shape[-1]
    rows = n // GRID_W
    k = 1.0 - f
    g = jnp.log(f)

    def chunks(t):
        return jnp.moveaxis(t.reshape(b, nh, rows, GRID_W, t.shape[-1]), 2, 0)

    causal = jnp.tril(jnp.ones((GRID_W, GRID_W), dtype=bool))[:, :, None]

    def step(S, inp):
        qc, kc, vc, gc = inp
        G = jnp.cumsum(gc, axis=-2)
        diff = G[..., :, None, :] - G[..., None, :, :]
        decay = jnp.exp(jnp.where(causal, diff, -jnp.inf))
        A = jnp.einsum('bhtd,bhsd,bhtsd->bhts', qc, kc, decay)
        o = jnp.einsum('bhts,bhse->bhte', A, vc) + jnp.einsum('bhtd,bhde->bhte', qc * jnp.exp(G), S)
        G_end = G[..., -1, :]
        S = jnp.exp(G_end)[..., :, None] * S + jnp.einsum(
            'bhsd,bhse->bhde', kc * jnp.exp(G_end[..., None, :] - G), vc)
        return S, o

    S_fin, o = lax.scan(step, s0, (chunks(q), chunks(k), chunks(v), chunks(g)))
    o = jnp.moveaxis(o, 0, 2).reshape(b, nh, n, dv)
    return o, S_fin


def hgrn2_bidir(q, zf, zb, v, lb_f, lb_b, sf0, sb0):
    qh = to_heads(jax.nn.silu(q)).astype(jnp.float32)
    vh = to_heads(v).astype(jnp.float32)
    lbf = lb_f.reshape(HGRN_HEADS, 1, HGRN_EXPAND)
    lbb = lb_b.reshape(HGRN_HEADS, 1, HGRN_EXPAND)
    ff = lbf + (1.0 - lbf) * jax.nn.sigmoid(to_heads(zf).astype(jnp.float32))
    fb = lbb + (1.0 - lbb) * jax.nn.sigmoid(to_heads(zb).astype(jnp.float32))
    o_f, sf = chunk_scan(qh, ff, vh, sf0)
    flip = lambda t: jnp.flip(t, axis=2)
    o_b, sb = chunk_scan(flip(qh), flip(fb), flip(vh), sb0)
    return o_f + flip(o_b), sf, sb


def mixer_scan(u, w_in, lb_f, lb_b, sf0, sb0):
    proj = u @ w_in
    q, zf, zb, v, gout, glu = jnp.split(
        proj, [HGRN_DIM, 2 * HGRN_DIM, 3 * HGRN_DIM, 4 * HGRN_DIM, 5 * HGRN_DIM], axis=-1)
    o, sf, sb = hgrn2_bidir(q, zf, zb, v, lb_f, lb_b, sf0, sb0)
    return o, gout, glu, sf, sb


def depthwise_conv(x, w, bias):
    pad = CONV_WIDTH // 2
    y = lax.conv_general_dilated(
        x, w[:, None, :].astype(x.dtype), window_strides=(1,), padding=[(pad, pad)],
        dimension_numbers=('NWC', 'WIO', 'NWC'), feature_group_count=x.shape[-1])
    return y + bias


def mixer_merge(u, o, gout, glu, hgrn_norm, w_proj_a, w_dw, b_dw, ln_g, ln_b, w_proj_b, w_gate, b_gate, w_o):
    b, n, _ = u.shape
    oa = rms_norm(o, hgrn_norm).transpose(0, 2, 1, 3).reshape(b, n, HGRN_DIM).astype(u.dtype)
    y_a = (oa * jax.nn.silu(gout)) @ w_proj_a
    ga, gb = jnp.split(glu, 2, axis=-1)
    hb = depthwise_conv(ga * jax.nn.sigmoid(gb), w_dw, b_dw)
    y_b = jax.nn.silu(layer_norm(hb, ln_g, ln_b)) @ w_proj_b
    gate_a, gate_b = jnp.split(jax.nn.sigmoid(u @ w_gate + b_gate), 2, axis=-1)
    return (gate_a * y_a + gate_b * y_b) @ w_o


def setup_inputs(seed: int = 0) -> dict:
    key = jax.random.key(seed)
    ks = jax.random.split(key, 32)
    nrm = lambda k, shape, s: jax.random.normal(k, shape, jnp.float32) * s
    D, L = D_MODEL, DEPTH
    lb_base = jnp.linspace(-1.0, 1.0, L + 1, dtype=jnp.float32)[:, None, None]
    return {
        "x": nrm(ks[0], (BATCH, SEQ, D), 1.0),
        "c": nrm(ks[1], (BATCH, D), 1.0),
        "ctx": nrm(ks[2], (BATCH, CTX_LEN, D), 1.0),
        "c_ctx": nrm(ks[3], (D,), 1.0),
        "w_ada": nrm(ks[4], (L, D, N_MOD * D), D ** -0.5),
        "b_ada": nrm(ks[5], (L, N_MOD * D), 0.02),
        "norm_ffn1": 1.0 + nrm(ks[6], (L, D), 0.05),
        "w_ffn1_up": nrm(ks[7], (L, D, 2 * FFN_DIM), D ** -0.5),
        "w_ffn1_down": nrm(ks[8], (L, FFN_DIM, D), FFN_DIM ** -0.5),
        "norm_mix": 1.0 + nrm(ks[9], (L, D), 0.05),
        "w_in": nrm(ks[10], (L, D, IN_COLS), D ** -0.5),
        "lb_logits": lb_base + nrm(ks[11], (L + 1, 2, HGRN_DIM), 0.1),
        "hgrn_norm": 1.0 + nrm(ks[12], (L, HGRN_HEAD_V), 0.05),
        "w_dw": nrm(ks[13], (L, CONV_WIDTH, CONV_DIM), CONV_WIDTH ** -0.5),
        "b_dw": nrm(ks[14], (L, CONV_DIM), 0.02),
        "conv_ln_g": 1.0 + nrm(ks[15], (L, CONV_DIM), 0.05),
        "conv_ln_b": nrm(ks[16], (L, CONV_DIM), 0.02),
        "w_proj_a": nrm(ks[17], (L, HGRN_DIM, D), HGRN_DIM ** -0.5),
        "w_proj_b": nrm(ks[18], (L, CONV_DIM, D), CONV_DIM ** -0.5),
        "w_gate": nrm(ks[19], (L, D, 2 * D), D ** -0.5),
        "b_gate": nrm(ks[20], (L, 2 * D), 0.02),
        "w_o": nrm(ks[21], (L, D, D), D ** -0.5),
        "norm_ffn2": 1.0 + nrm(ks[22], (L, D), 0.05),
        "w_ffn2_up": nrm(ks[23], (L, D, 2 * FFN_DIM), D ** -0.5),
        "w_ffn2_down": nrm(ks[24], (L, FFN_DIM, D), FFN_DIM ** -0.5),
        "final_norm": 1.0 + nrm(ks[25], (D,), 0.05),
    }


def reference(x, c, ctx, c_ctx, w_ada, b_ada, norm_ffn1, w_ffn1_up, w_ffn1_down, norm_mix, w_in,
              lb_logits, hgrn_norm, w_dw, b_dw, conv_ln_g, conv_ln_b, w_proj_a, w_proj_b, w_gate,
              b_gate, w_o, norm_ffn2, w_ffn2_up, w_ffn2_down, final_norm):
    b = x.shape[0]
    lb_all = jnp.cumsum(jax.nn.softmax(lb_logits.astype(jnp.float32), axis=0), axis=0)
    h, hc = x, ctx
    for l in range(DEPTH):
        last = l == DEPTH - 1
        m = jnp.split((jax.nn.silu(c) @ w_ada[l] + b_ada[l])[:, None, :], N_MOD, axis=-1)
        mc = jnp.split((jax.nn.silu(c_ctx) @ w_ada[l] + b_ada[l])[None, None, :], N_MOD, axis=-1)
        lb_f, lb_b = lb_all[l, 0], lb_all[l, 1]
        mix_w = (hgrn_norm[l], w_proj_a[l], w_dw[l], b_dw[l], conv_ln_g[l], conv_ln_b[l],
                 w_proj_b[l], w_gate[l], b_gate[l], w_o[l])
        h = ffn_sublayer(h, norm_ffn1[l], w_ffn1_up[l], w_ffn1_down[l], m[0], m[1], m[2])
        hc = ffn_sublayer(hc, norm_ffn1[l], w_ffn1_up[l], w_ffn1_down[l], mc[0], mc[1], mc[2])
        u = modulate(rms_norm(h, norm_mix[l]), m[3], m[4])
        uc = modulate(rms_norm(hc, norm_mix[l]), mc[3], mc[4])
        zeros = jnp.zeros((b, HGRN_HEADS, HGRN_EXPAND, HGRN_HEAD_V), jnp.float32)
        oc, goc, gluc, sf, sb = mixer_scan(uc, w_in[l], lb_f, lb_b, zeros, zeros)
        o, go, glu, _, _ = mixer_scan(u, w_in[l], lb_f, lb_b, sf, sb)
        h = h + m[5] * mixer_merge(u, o, go, glu, *mix_w)
        h = ffn_sublayer(h, norm_ffn2[l], w_ffn2_up[l], w_ffn2_down[l], m[6], m[7], m[8])
        if not last:
            hc = hc + mc[5] * mixer_merge(uc, oc, goc, gluc, *mix_w)
            hc = ffn_sublayer(hc, norm_ffn2[l], w_ffn2_up[l], w_ffn2_down[l], mc[6], mc[7], mc[8])
    return rms_norm(h, final_norm)
```

```python
import functools

import jax
import jax.numpy as jnp
from jax import lax
from jax.experimental import pallas as pl
from jax.experimental.pallas import tpu as pltpu

F32 = jnp.float32
BF16 = jnp.bfloat16

EPS = 1e-6
HEADS = 8
HEAD_DIM = 128
N_MOD = 9
CONV_WIDTH = 31
CONV_PAD = CONV_WIDTH // 2
HALO = 16
CHUNK = 64
SUB = 16
V7X_VMEM_LIMIT_BYTES = 56 * 1024 * 1024

NT_DIMS = (((1,), (1,)), ((), ()))
TN_DIMS = (((0,), (0,)), ((), ()))


def _params(*sem):
    return pltpu.CompilerParams(dimension_semantics=sem, vmem_limit_bytes=V7X_VMEM_LIMIT_BYTES)


def _const_spec(shape):
    nd = len(shape)
    return pl.BlockSpec(shape, lambda *_: (0,) * nd, pipeline_mode=pl.Buffered(1))


def _rms(x):
    return x * lax.rsqrt(jnp.mean(x * x, axis=-1, keepdims=True) + EPS)


def _dot(a, b):
    return jnp.dot(a, b, preferred_element_type=F32)


def _ada_kernel(c_ref, w_ref, b_ref, o_ref):
    a = c_ref[...]
    s = (a * jax.nn.sigmoid(a)).astype(BF16)
    o_ref[...] = _dot(s, w_ref[...].astype(BF16)) + b_ref[...]


def _ada(cc, w, b, *, tn=1024):
    rows, d = cc.shape
    n = w.shape[1]
    return pl.pallas_call(
        _ada_kernel,
        grid=(n // tn,),
        in_specs=[pl.BlockSpec((rows, d), lambda j: (0, 0)),
                  pl.BlockSpec((d, tn), lambda j: (0, j)),
                  pl.BlockSpec((1, tn), lambda j: (0, j))],
        out_specs=pl.BlockSpec((rows, tn), lambda j: (0, j)),
        out_shape=jax.ShapeDtypeStruct((rows, n), F32),
        compiler_params=_params("parallel"),
        name="ada",
    )(cc, w, b.reshape(1, n))


def _ffn_kernel(h_ref, sh_ref, sc_ref, gt_ref, gain_ref, wa_ref, wb_ref, wd_ref, fg_ref, o_ref,
                *, f_chunk, final):
    h = h_ref[0]
    u = (_rms(h) * gain_ref[...]) * (1.0 + sc_ref[0]) + sh_ref[0]
    ub = u.astype(BF16)
    f_dim = wa_ref.shape[1]
    y = None
    for c in range(f_dim // f_chunk):
        sl = slice(c * f_chunk, (c + 1) * f_chunk)
        a = _dot(ub, wa_ref[:, sl])
        b = _dot(ub, wb_ref[:, sl])
        act = (a * jax.nn.sigmoid(a) * b).astype(BF16)
        yc = _dot(act, wd_ref[sl, :])
        y = yc if y is None else y + yc
    out = h + (0.5 * gt_ref[0]) * y
    if final:
        out = _rms(out) * fg_ref[...]
    o_ref[0] = out


def _ffn(h, shift, scale, gate, gain, wa, wb, wd, final_gain=None, *, tm):
    bsz, t, d = h.shape
    f_dim = wa.shape[1]
    per_batch = shift.shape[0] == bsz
    mod_spec = pl.BlockSpec((1, 1, d), (lambda b, i: (b, 0, 0)) if per_batch else (lambda b, i: (0, 0, 0)))
    final = final_gain is not None
    fg = final_gain if final else gain
    tok_spec = pl.BlockSpec((1, tm, d), lambda b, i: (b, i, 0))
    return pl.pallas_call(
        functools.partial(_ffn_kernel, f_chunk=f_dim // 2, final=final),
        grid=(bsz, t // tm),
        in_specs=[tok_spec, mod_spec, mod_spec, mod_spec, _const_spec((1, d)),
                  _const_spec((d, f_dim)), _const_spec((d, f_dim)), _const_spec((f_dim, d)),
                  _const_spec((1, d))],
        out_specs=tok_spec,
        out_shape=jax.ShapeDtypeStruct((bsz, t, d), F32),
        compiler_params=_params("parallel", "parallel"),
        name="ffn_final" if final else "ffn",
    )(h, shift, scale, gate, gain.reshape(1, d), wa, wb, wd, fg.reshape(1, d))


def _proj_kernel(h_ref, sh_ref, sc_ref, gain_ref, w_ref,
                 q_ref, zf_ref, zb_ref, v_ref, go_ref, glu_ref):
    h = h_ref[0]
    d = h.shape[-1]
    u = ((_rms(h) * gain_ref[...]) * (1.0 + sc_ref[0]) + sh_ref[0]).astype(BF16)

    def col(g):
        return _dot(u, w_ref[:, g * d:(g + 1) * d])

    q = col(0)
    q_ref[0] = (q * jax.nn.sigmoid(q)).astype(BF16)
    zf_ref[0] = col(1)
    zb_ref[0] = col(2)
    v_ref[0] = col(3).astype(BF16)
    go = col(4)
    go_ref[0] = (go * jax.nn.sigmoid(go)).astype(BF16)
    glu_ref[0] = col(5) * jax.nn.sigmoid(col(6))


def _proj(h, shift, scale, gain, w_in, *, tm):
    bsz, t, d = h.shape
    per_batch = shift.shape[0] == bsz
    mod_spec = pl.BlockSpec((1, 1, d), (lambda b, i: (b, 0, 0)) if per_batch else (lambda b, i: (0, 0, 0)))
    tok_spec = pl.BlockSpec((1, tm, d), lambda b, i: (b, i, 0))
    sds = lambda dt: jax.ShapeDtypeStruct((bsz, t, d), dt)
    return pl.pallas_call(
        _proj_kernel,
        grid=(bsz, t // tm),
        in_specs=[tok_spec, mod_spec, mod_spec, _const_spec((1, d)), _const_spec(w_in.shape)],
        out_specs=[tok_spec] * 6,
        out_shape=[sds(BF16), sds(F32), sds(F32), sds(BF16), sds(BF16), sds(F32)],
        compiler_params=_params("parallel", "parallel"),
        name="proj",
    )(h, shift, scale, gain.reshape(1, d), w_in)


def _scan_dir(direction, c, n_chunks, q_ref, z_ref, v_ref, lb, o_ref, st_ref):
    fwd = direction == 0
    cc = c if fwd else n_chunks - 1 - c
    r0 = pl.multiple_of(cc * CHUNK, CHUNK)
    rows = pl.ds(r0, CHUNK)

    z = z_ref[0, rows, :]
    f = lb + (1.0 - lb) * jax.nn.sigmoid(z)
    g = jnp.log(f)
    k = 1.0 - f

    ri = lax.broadcasted_iota(jnp.int32, (CHUNK, CHUNK), 0)
    ci = lax.broadcasted_iota(jnp.int32, (CHUNK, CHUNK), 1)
    tri = jnp.where((ci <= ri) if fwd else (ci >= ri), 1.0, 0.0).astype(BF16)
    g1 = g.astype(BF16)
    rem = g - g1.astype(F32)
    g2 = rem.astype(BF16)
    g3 = (rem - g2.astype(F32)).astype(BF16)
    G = _dot(tri, g1) + _dot(tri, g2) + _dot(tri, g3)

    q = q_ref[0, rows, :].astype(F32)
    v = v_ref[0, rows, :]
    gend = G[CHUNK - 1:CHUNK] if fwd else G[0:1]
    qt = (q * jnp.exp(G)).astype(BF16)
    kt = (k * jnp.exp(gend - G)).astype(BF16)
    dec = jnp.exp(gend)

    for h in range(HEADS):
        hl = slice(h * HEAD_DIM, (h + 1) * HEAD_DIM)
        st = st_ref[direction, h]
        o_inter = lax.dot_general(qt[:, hl], st.astype(BF16), NT_DIMS, preferred_element_type=F32)
        Gh, qh, kh, vh = G[:, hl], q[:, hl], k[:, hl], v[:, hl]
        for i in range(CHUNK // SUB):
            rs = slice(SUB * i, SUB * (i + 1))
            lo, hi = (0, SUB * (i + 1)) if fwd else (SUB * i, CHUNK)
            mid = SUB * i + (SUB // 2 - 1 if fwd else SUB // 2)
            rm = Gh[mid:mid + 1]
            qi = (qh[rs] * jnp.exp(Gh[rs] - rm)).astype(BF16)
            ki = (kh[lo:hi] * jnp.exp(rm - Gh[lo:hi])).astype(BF16)
            a = lax.dot_general(qi, ki, NT_DIMS, preferred_element_type=F32)
            rowg = SUB * i + lax.broadcasted_iota(jnp.int32, a.shape, 0)
            colg = lo + lax.broadcasted_iota(jnp.int32, a.shape, 1)
            keep = (colg <= rowg) if fwd else (colg >= rowg)
            a = jnp.where(keep, a, 0.0).astype(BF16)
            o_i = _dot(a, vh[lo:hi]) + o_inter[rs]
            o_ref[0, pl.ds(r0 + SUB * i, SUB), hl] = o_i
        upd = lax.dot_general(vh, kt[:, hl], TN_DIMS, preferred_element_type=F32)
        st_ref[direction, h] = st * dec[:, hl] + upd


def _scan_kernel(qf_ref, zf_ref, vf_ref, qb_ref, zb_ref, vb_ref, lb_ref, s0_ref,
                 of_ref, ob_ref, sfin_ref, st_ref, *, n_chunks):
    j = pl.program_id(1)

    @pl.when(j == 0)
    def _():
        st_ref[...] = s0_ref[0]

    lb_f = lb_ref[0:1, :]
    lb_b = lb_ref[1:2, :]

    def body(c, carry):
        _scan_dir(0, c, n_chunks, qf_ref, zf_ref, vf_ref, lb_f, of_ref, st_ref)
        _scan_dir(1, c, n_chunks, qb_ref, zb_ref, vb_ref, lb_b, ob_ref, st_ref)
        return carry

    lax.fori_loop(0, n_chunks, body, 0)

    @pl.when(j == pl.num_programs(1) - 1)
    def _():
        sfin_ref[0] = st_ref[...]


def _scan(q, zf, zb, v, lb, s0, *, tb):
    bsz, t, d = q.shape
    nb = t // tb
    fwd_spec = pl.BlockSpec((1, tb, d), lambda b, j: (b, j, 0))
    bwd_spec = pl.BlockSpec((1, tb, d), lambda b, j: (b, nb - 1 - j, 0))
    st_shape = (2, HEADS, HEAD_DIM, HEAD_DIM)
    st_spec = pl.BlockSpec((1,) + st_shape, lambda b, j: (b, 0, 0, 0, 0))
    return pl.pallas_call(
        functools.partial(_scan_kernel, n_chunks=tb // CHUNK),
        grid=(bsz, nb),
        in_specs=[fwd_spec, fwd_spec, fwd_spec, bwd_spec, bwd_spec, bwd_spec,
                  pl.BlockSpec((2, d), lambda b, j: (0, 0)), st_spec],
        out_specs=[fwd_spec, bwd_spec, st_spec],
        out_shape=[jax.ShapeDtypeStruct((bsz, t, d), F32), jax.ShapeDtypeStruct((bsz, t, d), F32),
                   jax.ShapeDtypeStruct((bsz,) + st_shape, F32)],
        scratch_shapes=[pltpu.VMEM(st_shape, F32)],
        compiler_params=_params("parallel", "arbitrary"),
        name="scan",
    )(q, zf, v, q, zb, v, lb, s0)


def _merge_kernel(h_ref, sh_ref, sc_ref, gt_ref, of_ref, ob_ref, go_ref, glu_ref, prev_ref, next_ref,
                  gain_ref, hn_ref, wdw_ref, bdw_ref, lng_ref, lnb_ref, wpa_ref, wpb_ref,
                  wg_ref, bg_ref, wo_ref, o_ref, xa_ref, xp_ref, *, tm):
    i = pl.program_id(1)
    h = h_ref[0]
    d = h.shape[-1]
    u = ((_rms(h) * gain_ref[...]) * (1.0 + sc_ref[0]) + sh_ref[0]).astype(BF16)
    gates = jax.nn.sigmoid(_dot(u, wg_ref[...]) + bg_ref[...])

    o = of_ref[0] + ob_ref[0]
    for hd in range(HEADS):
        hl = slice(hd * HEAD_DIM, (hd + 1) * HEAD_DIM)
        oh = _rms(o[:, hl]) * hn_ref[...]
        xa_ref[:, hl] = (oh.astype(F32) * go_ref[0, :, hl].astype(F32)).astype(BF16)
    y_a = _dot(xa_ref[...], wpa_ref[...])

    xp_ref[0:HALO, :] = jnp.where(i > 0, prev_ref[0], 0.0)
    xp_ref[HALO:HALO + tm, :] = glu_ref[0]
    xp_ref[HALO + tm:2 * HALO + tm, :] = jnp.where(i < pl.num_programs(1) - 1, next_ref[0], 0.0)
    hb = None
    for j in range(CONV_WIDTH):
        off = HALO - CONV_PAD + j
        term = xp_ref[off:off + tm, :] * wdw_ref[j:j + 1, :]
        hb = term if hb is None else hb + term
    hb = hb + bdw_ref[...]
    mu = jnp.mean(hb, axis=-1, keepdims=True)
    cen = hb - mu
    var = jnp.mean(cen * cen, axis=-1, keepdims=True)
    ln = cen * lax.rsqrt(var + EPS) * lng_ref[...] + lnb_ref[...]
    y_b = _dot((ln * jax.nn.sigmoid(ln)).astype(BF16), wpb_ref[...])

    mixed = (gates[:, :d] * y_a + gates[:, d:] * y_b).astype(BF16)
    o_ref[0] = h + gt_ref[0] * _dot(mixed, wo_ref[...])


def _merge(h, shift, scale, gate, o_f, o_b, gos, glu, gain, hgrn_norm, w_dw, b_dw, ln_g, ln_b,
           w_pa, w_pb, w_gate, b_gate, w_o, *, tm):
    bsz, t, d = h.shape
    nt = t // tm
    per_tile = tm // HALO
    n_halo = t // HALO
    tok_spec = pl.BlockSpec((1, tm, d), lambda b, i: (b, i, 0))
    mod_spec = pl.BlockSpec((1, 1, d), lambda b, i: (b, 0, 0))
    prev_spec = pl.BlockSpec((1, HALO, d), lambda b, i: (b, jnp.maximum(i * per_tile - 1, 0), 0))
    next_spec = pl.BlockSpec((1, HALO, d), lambda b, i: (b, jnp.minimum((i + 1) * per_tile, n_halo - 1), 0))
    row = lambda a: a.reshape(1, -1)
    return pl.pallas_call(
        functools.partial(_merge_kernel, tm=tm),
        grid=(bsz, nt),
        in_specs=[tok_spec, mod_spec, mod_spec, mod_spec, tok_spec, tok_spec, tok_spec, tok_spec,
                  prev_spec, next_spec,
                  _const_spec((1, d)), _const_spec((1, HEAD_DIM)), _const_spec((CONV_WIDTH, d)),
                  _const_spec((1, d)), _const_spec((1, d)), _const_spec((1, d)),
                  _const_spec((d, d)), _const_spec((d, d)), _const_spec((d, 2 * d)),
                  _const_spec((1, 2 * d)), _const_spec((d, d))],
        out_specs=tok_spec,
        out_shape=jax.ShapeDtypeStruct((bsz, t, d), F32),
        scratch_shapes=[pltpu.VMEM((tm, d), BF16), pltpu.VMEM((tm + 2 * HALO, d), F32)],
        compiler_params=_params("parallel", "parallel"),
        name="merge",
    )(h, shift, scale, gate, o_f, o_b, gos, glu, glu, glu,
      row(gain), row(hgrn_norm), w_dw, row(b_dw), row(ln_g), row(ln_b), w_pa, w_pb, w_gate, row(b_gate), w_o)


def _tile(t, want):
    tm = min(t, want)
    assert t % tm == 0
    return tm


def kernel(x, c, ctx, c_ctx, w_ada, b_ada, norm_ffn1, w_ffn1_up, w_ffn1_down, norm_mix, w_in, lb_logits,
           hgrn_norm, w_dw, b_dw, conv_ln_g, conv_ln_b, w_proj_a, w_proj_b, w_gate, b_gate, w_o, norm_ffn2,
           w_ffn2_up, w_ffn2_down, final_norm):
    bsz, seq, d = x.shape
    depth = w_ada.shape[0]
    assert depth == 1 and d == HEADS * HEAD_DIM
    f_dim = w_ffn1_down.shape[1]
    l = 0

    lb = jnp.cumsum(jax.nn.softmax(lb_logits.astype(F32), axis=0), axis=0)[l]

    n_rows = -(-(bsz + 1) // 8) * 8
    cc = jnp.zeros((n_rows, d), F32).at[:bsz].set(c).at[bsz].set(c_ctx)
    mods = _ada(cc, w_ada[l], b_ada[l])
    m = mods[:bsz].reshape(bsz, N_MOD, 1, d)
    mc = mods[bsz:bsz + 1].reshape(1, N_MOD, 1, d)
    mod = lambda mm, idx: mm[:, idx]

    bf = lambda w: w.astype(BF16)
    w1a, w1b, w1d = bf(w_ffn1_up[l][:, :f_dim]), bf(w_ffn1_up[l][:, f_dim:]), bf(w_ffn1_down[l])
    w2a, w2b, w2d = bf(w_ffn2_up[l][:, :f_dim]), bf(w_ffn2_up[l][:, f_dim:]), bf(w_ffn2_down[l])
    w_in_b = bf(w_in[l])

    tm = _tile(seq, 512)
    tmc = _tile(ctx.shape[1], 256)

    h = _ffn(x, mod(m, 0), mod(m, 1), mod(m, 2), norm_ffn1[l], w1a, w1b, w1d, tm=tm)
    hc = _ffn(ctx, mod(mc, 0), mod(mc, 1), mod(mc, 2), norm_ffn1[l], w1a, w1b, w1d, tm=tmc)

    qc, zfc, zbc, vc, _, _ = _proj(hc, mod(mc, 3), mod(mc, 4), norm_mix[l], w_in_b, tm=tmc)
    q, zf, zb, v, gos, glu = _proj(h, mod(m, 3), mod(m, 4), norm_mix[l], w_in_b, tm=tm)

    s0 = jnp.zeros((bsz, 2, HEADS, HEAD_DIM, HEAD_DIM), F32)
    _, _, s_ctx = _scan(qc, zfc, zbc, vc, lb, s0, tb=_tile(ctx.shape[1], 256))
    o_f, o_b, _ = _scan(q, zf, zb, v, lb, s_ctx, tb=_tile(seq, 256))

    h = _merge(h, mod(m, 3), mod(m, 4), mod(m, 5), o_f, o_b, gos, glu, norm_mix[l], hgrn_norm[l],
               w_dw[l], b_dw[l], conv_ln_g[l], conv_ln_b[l], bf(w_proj_a[l]), bf(w_proj_b[l]),
               bf(w_gate[l]), b_gate[l], bf(w_o[l]), tm=_tile(seq, 256))

    return _ffn(h, mod(m, 6), mod(m, 7), mod(m, 8), norm_ffn2[l], w2a, w2b, w2d, final_norm, tm=tm)
```

```python
import functools

import jax
import jax.numpy as jnp
from jax import lax
from jax.experimental import pallas as pl
from jax.experimental.pallas import tpu as pltpu

F32 = jnp.float32
BF16 = jnp.bfloat16

EPS = 1e-6
HEADS = 8
HEAD_DIM = 128
N_MOD = 9
CONV_WIDTH = 31
CONV_PAD = CONV_WIDTH // 2
HALO = 16
CHUNK = 64
SUB = 16
V7X_VMEM_LIMIT_BYTES = 56 * 1024 * 1024

NT_DIMS = (((1,), (1,)), ((), ()))
TN_DIMS = (((0,), (0,)), ((), ()))


def _params(*sem):
    return pltpu.CompilerParams(dimension_semantics=sem, vmem_limit_bytes=V7X_VMEM_LIMIT_BYTES)


def _const_spec(shape):
    nd = len(shape)
    return pl.BlockSpec(shape, lambda *_: (0,) * nd, pipeline_mode=pl.Buffered(1))


def _rms(x):
    return x * lax.rsqrt(jnp.mean(x * x, axis=-1, keepdims=True) + EPS)


def _dot(a, b):
    return jnp.dot(a, b, preferred_element_type=F32)


def _ada_kernel(c_ref, w_ref, b_ref, o_ref):
    a = c_ref[...]
    s = (a * jax.nn.sigmoid(a)).astype(BF16)
    o_ref[...] = _dot(s, w_ref[...].astype(BF16)) + b_ref[...]


def _ada(cc, w, b, *, tn=1024):
    rows, d = cc.shape
    n = w.shape[1]
    return pl.pallas_call(
        _ada_kernel,
        grid=(n // tn,),
        in_specs=[pl.BlockSpec((rows, d), lambda j: (0, 0)),
                  pl.BlockSpec((d, tn), lambda j: (0, j)),
                  pl.BlockSpec((1, tn), lambda j: (0, j))],
        out_specs=pl.BlockSpec((rows, tn), lambda j: (0, j)),
        out_shape=jax.ShapeDtypeStruct((rows, n), F32),
        compiler_params=_params("parallel"),
        name="ada",
    )(cc, w, b.reshape(1, n))


def _ffn_kernel(h_ref, sh_ref, sc_ref, gt_ref, gain_ref, wa_ref, wb_ref, wd_ref, fg_ref, o_ref,
                *, f_chunk, final):
    h = h_ref[0]
    u = (_rms(h) * gain_ref[...]) * (1.0 + sc_ref[0]) + sh_ref[0]
    ub = u.astype(BF16)
    f_dim = wa_ref.shape[1]
    y = None
    for c in range(f_dim // f_chunk):
        sl = slice(c * f_chunk, (c + 1) * f_chunk)
        a = _dot(ub, wa_ref[:, sl])
        b = _dot(ub, wb_ref[:, sl])
        act = (a * jax.nn.sigmoid(a) * b).astype(BF16)
        yc = _dot(act, wd_ref[sl, :])
        y = yc if y is None else y + yc
    out = h + (0.5 * gt_ref[0]) * y
    if final:
        out = _rms(out) * fg_ref[...]
    o_ref[0] = out


def _ffn(h, shift, scale, gate, gain, wa, wb, wd, final_gain=None, *, tm):
    bsz, t, d = h.shape
    f_dim = wa.shape[1]
    per_batch = shift.shape[0] == bsz
    mod_spec = pl.BlockSpec((1, 1, d), (lambda b, i: (b, 0, 0)) if per_batch else (lambda b, i: (0, 0, 0)))
    final = final_gain is not None
    fg = final_gain if final else gain
    tok_spec = pl.BlockSpec((1, tm, d), lambda b, i: (b, i, 0))
    return pl.pallas_call(
        functools.partial(_ffn_kernel, f_chunk=f_dim // 2, final=final),
        grid=(bsz, t // tm),
        in_specs=[tok_spec, mod_spec, mod_spec, mod_spec, _const_spec((1, d)),
                  _const_spec((d, f_dim)), _const_spec((d, f_dim)), _const_spec((f_dim, d)),
                  _const_spec((1, d))],
        out_specs=tok_spec,
        out_shape=jax.ShapeDtypeStruct((bsz, t, d), F32),
        compiler_params=_params("parallel", "parallel"),
        name="ffn_final" if final else "ffn",
    )(h, shift, scale, gate, gain.reshape(1, d), wa, wb, wd, fg.reshape(1, d))


def _chunk_cumsum(g, reverse):
    rows, n = g.shape
    x = g.reshape(rows // 8, 8, n)
    sub = lax.broadcasted_iota(jnp.int32, x.shape, 1)
    for s in (1, 2, 4):
        if reverse:
            x = x + jnp.where(sub < 8 - s, pltpu.roll(x, 8 - s, axis=1), 0.0)
        else:
            x = x + jnp.where(sub >= s, pltpu.roll(x, s, axis=1), 0.0)
    per = CHUNK // 8
    out = []
    for c in range(rows // CHUNK):
        blk = [None] * per
        run = None
        for j in (reversed(range(per)) if reverse else range(per)):
            grp = x[c * per + j]
            blk[j] = grp if run is None else grp + run
            tot = grp[0:1] if reverse else grp[7:8]
            run = tot if run is None else run + tot
        out.extend(blk)
    return jnp.concatenate(out, axis=0)


def _proj_kernel(h_ref, sh_ref, sc_ref, gain_ref, lb_ref, w_ref,
                 q_ref, gf_ref, gb_ref, kf_ref, kb_ref, v_ref, go_ref, glu_ref):
    h = h_ref[0]
    d = h.shape[-1]
    u = ((_rms(h) * gain_ref[...]) * (1.0 + sc_ref[0]) + sh_ref[0]).astype(BF16)

    def col(g):
        return _dot(u, w_ref[:, g * d:(g + 1) * d])

    q = col(0)
    q_ref[0] = (q * jax.nn.sigmoid(q)).astype(BF16)
    for direction, (g_ref, k_ref) in enumerate(((gf_ref, kf_ref), (gb_ref, kb_ref))):
        lb = lb_ref[direction:direction + 1, :]
        f = lb + (1.0 - lb) * jax.nn.sigmoid(col(1 + direction))
        k_ref[0] = (1.0 - f).astype(BF16)
        g_ref[0] = _chunk_cumsum(jnp.log(f), reverse=direction == 1)
    v_ref[0] = col(3).astype(BF16)
    go = col(4)
    go_ref[0] = (go * jax.nn.sigmoid(go)).astype(BF16)
    glu_ref[0] = col(5) * jax.nn.sigmoid(col(6))


def _proj(h, shift, scale, gain, lb, w_in, *, tm):
    bsz, t, d = h.shape
    per_batch = shift.shape[0] == bsz
    mod_spec = pl.BlockSpec((1, 1, d), (lambda b, i: (b, 0, 0)) if per_batch else (lambda b, i: (0, 0, 0)))
    tok_spec = pl.BlockSpec((1, tm, d), lambda b, i: (b, i, 0))
    sds = lambda dt: jax.ShapeDtypeStruct((bsz, t, d), dt)
    return pl.pallas_call(
        _proj_kernel,
        grid=(bsz, t // tm),
        in_specs=[tok_spec, mod_spec, mod_spec, _const_spec((1, d)), _const_spec((2, d)), _const_spec(w_in.shape)],
        out_specs=[tok_spec] * 8,
        out_shape=[sds(BF16), sds(F32), sds(F32), sds(BF16), sds(BF16), sds(BF16), sds(BF16), sds(F32)],
        compiler_params=_params("parallel", "parallel"),
        name="proj",
    )(h, shift, scale, gain.reshape(1, d), lb, w_in)


def _scan_local(direction, c, q_ref, g_ref, k_ref, v_ref, o_ref, qt_ref, u_ref, keep):
    fwd = direction == 0
    r0 = pl.multiple_of(c * CHUNK, CHUNK)
    rows = pl.ds(r0, CHUNK)
    G = g_ref[0, rows, :]
    k = k_ref[0, rows, :].astype(F32)
    q = q_ref[0, rows, :].astype(F32)
    v = v_ref[0, rows, :]
    gend = G[CHUNK - 1:CHUNK] if fwd else G[0:1]
    qt_ref[direction, rows, :] = (q * jnp.exp(G)).astype(BF16)
    kt = (k * jnp.exp(gend - G)).astype(BF16)

    n_sub = CHUNK // SUB
    qis, kis = [], []
    for i in range(n_sub):
        rs = slice(SUB * i, SUB * (i + 1))
        lo, hi = (0, SUB * (i + 1)) if fwd else (SUB * i, CHUNK)
        mid = SUB * i + (SUB // 2 - 1 if fwd else SUB // 2)
        rm = G[mid:mid + 1]
        qis.append((q[rs] * jnp.exp(G[rs] - rm)).astype(BF16))
        parts = [k[lo:hi] * jnp.exp(rm - G[lo:hi])]
        if lo > 0:
            parts.insert(0, jnp.zeros((lo, k.shape[1]), F32))
        if hi < CHUNK:
            parts.append(jnp.zeros((CHUNK - hi, k.shape[1]), F32))
        kis.append(jnp.concatenate(parts, axis=0).astype(BF16))

    heads = [slice(h * HEAD_DIM, (h + 1) * HEAD_DIM) for h in range(HEADS)]
    a = [[lax.dot_general(qis[i][:, hl], kis[i][:, hl], NT_DIMS, preferred_element_type=F32)
          for i in range(n_sub)] for hl in heads]
    for h, hl in enumerate(heads):
        u_ref[direction, c, h] = lax.dot_general(v[:, hl], kt[:, hl], TN_DIMS, preferred_element_type=F32)
    am = [jnp.where(keep, jnp.concatenate(a[h], axis=0), 0.0).astype(BF16) for h in range(HEADS)]
    for h, hl in enumerate(heads):
        o_ref[0, rows, hl] = _dot(am[h], v[:, hl])


def _scan_carry(direction, c, n_chunks, g_ref, o_ref, qt_ref, u_ref, st_ref):
    fwd = direction == 0
    cc = c if fwd else n_chunks - 1 - c
    r0 = pl.multiple_of(cc * CHUNK, CHUNK)
    rows = pl.ds(r0, CHUNK)
    dec = jnp.exp(g_ref[0, pl.ds(r0 + (CHUNK - 1 if fwd else 0), 1), :])
    for h in range(HEADS):
        hl = slice(h * HEAD_DIM, (h + 1) * HEAD_DIM)
        st = st_ref[direction, h]
        o_ref[0, rows, hl] += lax.dot_general(qt_ref[direction, rows, hl], st.astype(BF16), NT_DIMS,
                                              preferred_element_type=F32)
        st_ref[direction, h] = st * dec[:, hl] + u_ref[direction, cc, h]


def _scan_kernel(qf_ref, gf_ref, kf_ref, vf_ref, qb_ref, gb_ref, kb_ref, vb_ref, s0_ref,
                 of_ref, ob_ref, sfin_ref, st_ref, qt_ref, u_ref, *, n_chunks):
    j = pl.program_id(1)

    @pl.when(j == 0)
    def _():
        st_ref[...] = s0_ref[0]

    ri = lax.broadcasted_iota(jnp.int32, (CHUNK, CHUNK), 0)
    ci = lax.broadcasted_iota(jnp.int32, (CHUNK, CHUNK), 1)
    keep_f, keep_b = ci <= ri, ci >= ri

    def local(c, carry):
        _scan_local(0, c, qf_ref, gf_ref, kf_ref, vf_ref, of_ref, qt_ref, u_ref, keep_f)
        _scan_local(1, c, qb_ref, gb_ref, kb_ref, vb_ref, ob_ref, qt_ref, u_ref, keep_b)
        return carry

    lax.fori_loop(0, n_chunks, local, 0)

    def carry_step(c, carry):
        _scan_carry(0, c, n_chunks, gf_ref, of_ref, qt_ref, u_ref, st_ref)
        _scan_carry(1, c, n_chunks, gb_ref, ob_ref, qt_ref, u_ref, st_ref)
        return carry

    lax.fori_loop(0, n_chunks, carry_step, 0)

    @pl.when(j == pl.num_programs(1) - 1)
    def _():
        sfin_ref[0] = st_ref[...]


def _scan(q, gf, gb, kf, kb, v, s0, *, tb):
    bsz, t, d = q.shape
    nb = t // tb
    n_chunks = tb // CHUNK
    fwd_spec = pl.BlockSpec((1, tb, d), lambda b, j: (b, j, 0))
    bwd_spec = pl.BlockSpec((1, tb, d), lambda b, j: (b, nb - 1 - j, 0))
    st_shape = (2, HEADS, HEAD_DIM, HEAD_DIM)
    st_spec = pl.BlockSpec((1,) + st_shape, lambda b, j: (b, 0, 0, 0, 0))
    return pl.pallas_call(
        functools.partial(_scan_kernel, n_chunks=n_chunks),
        grid=(bsz, nb),
        in_specs=[fwd_spec] * 4 + [bwd_spec] * 4 + [st_spec],
        out_specs=[fwd_spec, bwd_spec, st_spec],
        out_shape=[jax.ShapeDtypeStruct((bsz, t, d), F32), jax.ShapeDtypeStruct((bsz, t, d), F32),
                   jax.ShapeDtypeStruct((bsz,) + st_shape, F32)],
        scratch_shapes=[pltpu.VMEM(st_shape, F32), pltpu.VMEM((2, tb, d), BF16),
                        pltpu.VMEM((2, n_chunks, HEADS, HEAD_DIM, HEAD_DIM), F32)],
        compiler_params=_params("parallel", "arbitrary"),
        name="scan",
    )(q, gf, kf, v, q, gb, kb, v, s0)


def _merge_kernel(h_ref, sh_ref, sc_ref, gt_ref, of_ref, ob_ref, go_ref, glu_ref, prev_ref, next_ref,
                  gain_ref, hn_ref, wdw_ref, bdw_ref, lng_ref, lnb_ref, wpa_ref, wpb_ref,
                  wg_ref, bg_ref, wo_ref, o_ref, xa_ref, xp_ref, *, tm):
    i = pl.program_id(1)
    h = h_ref[0]
    d = h.shape[-1]
    u = ((_rms(h) * gain_ref[...]) * (1.0 + sc_ref[0]) + sh_ref[0]).astype(BF16)
    gates = jax.nn.sigmoid(_dot(u, wg_ref[...]) + bg_ref[...])

    o = of_ref[0] + ob_ref[0]
    for hd in range(HEADS):
        hl = slice(hd * HEAD_DIM, (hd + 1) * HEAD_DIM)
        oh = _rms(o[:, hl]) * hn_ref[...]
        xa_ref[:, hl] = (oh.astype(F32) * go_ref[0, :, hl].astype(F32)).astype(BF16)
    y_a = _dot(xa_ref[...], wpa_ref[...])

    xp_ref[0:HALO, :] = jnp.where(i > 0, prev_ref[0], 0.0)
    xp_ref[HALO:HALO + tm, :] = glu_ref[0]
    xp_ref[HALO + tm:2 * HALO + tm, :] = jnp.where(i < pl.num_programs(1) - 1, next_ref[0], 0.0)
    hb = None
    for j in range(CONV_WIDTH):
        off = HALO - CONV_PAD + j
        term = xp_ref[off:off + tm, :] * wdw_ref[j:j + 1, :]
        hb = term if hb is None else hb + term
    hb = hb + bdw_ref[...]
    mu = jnp.mean(hb, axis=-1, keepdims=True)
    cen = hb - mu
    var = jnp.mean(cen * cen, axis=-1, keepdims=True)
    ln = cen * lax.rsqrt(var + EPS) * lng_ref[...] + lnb_ref[...]
    y_b = _dot((ln * jax.nn.sigmoid(ln)).astype(BF16), wpb_ref[...])

    mixed = (gates[:, :d] * y_a + gates[:, d:] * y_b).astype(BF16)
    o_ref[0] = h + gt_ref[0] * _dot(mixed, wo_ref[...])


def _merge(h, shift, scale, gate, o_f, o_b, gos, glu, gain, hgrn_norm, w_dw, b_dw, ln_g, ln_b,
           w_pa, w_pb, w_gate, b_gate, w_o, *, tm):
    bsz, t, d = h.shape
    nt = t // tm
    per_tile = tm // HALO
    n_halo = t // HALO
    tok_spec = pl.BlockSpec((1, tm, d), lambda b, i: (b, i, 0))
    mod_spec = pl.BlockSpec((1, 1, d), lambda b, i: (b, 0, 0))
    prev_spec = pl.BlockSpec((1, HALO, d), lambda b, i: (b, jnp.maximum(i * per_tile - 1, 0), 0))
    next_spec = pl.BlockSpec((1, HALO, d), lambda b, i: (b, jnp.minimum((i + 1) * per_tile, n_halo - 1), 0))
    row = lambda a: a.reshape(1, -1)
    return pl.pallas_call(
        functools.partial(_merge_kernel, tm=tm),
        grid=(bsz, nt),
        in_specs=[tok_spec, mod_spec, mod_spec, mod_spec, tok_spec, tok_spec, tok_spec, tok_spec,
                  prev_spec, next_spec,
                  _const_spec((1, d)), _const_spec((1, HEAD_DIM)), _const_spec((CONV_WIDTH, d)),
                  _const_spec((1, d)), _const_spec((1, d)), _const_spec((1, d)),
                  _const_spec((d, d)), _const_spec((d, d)), _const_spec((d, 2 * d)),
                  _const_spec((1, 2 * d)), _const_spec((d, d))],
        out_specs=tok_spec,
        out_shape=jax.ShapeDtypeStruct((bsz, t, d), F32),
        scratch_shapes=[pltpu.VMEM((tm, d), BF16), pltpu.VMEM((tm + 2 * HALO, d), F32)],
        compiler_params=_params("parallel", "parallel"),
        name="merge",
    )(h, shift, scale, gate, o_f, o_b, gos, glu, glu, glu,
      row(gain), row(hgrn_norm), w_dw, row(b_dw), row(ln_g), row(ln_b), w_pa, w_pb, w_gate, row(b_gate), w_o)


def _tile(t, want):
    tm = min(t, want)
    assert t % tm == 0
    return tm


def kernel(x, c, ctx, c_ctx, w_ada, b_ada, norm_ffn1, w_ffn1_up, w_ffn1_down, norm_mix, w_in, lb_logits,
           hgrn_norm, w_dw, b_dw, conv_ln_g, conv_ln_b, w_proj_a, w_proj_b, w_gate, b_gate, w_o, norm_ffn2,
           w_ffn2_up, w_ffn2_down, final_norm):
    bsz, seq, d = x.shape
    depth = w_ada.shape[0]
    assert depth == 1 and d == HEADS * HEAD_DIM
    f_dim = w_ffn1_down.shape[1]
    l = 0

    lb = jnp.cumsum(jax.nn.softmax(lb_logits.astype(F32), axis=0), axis=0)[l]

    n_rows = -(-(bsz + 1) // 8) * 8
    cc = jnp.zeros((n_rows, d), F32).at[:bsz].set(c).at[bsz].set(c_ctx)
    mods = _ada(cc, w_ada[l], b_ada[l])
    m = mods[:bsz].reshape(bsz, N_MOD, 1, d)
    mc = mods[bsz:bsz + 1].reshape(1, N_MOD, 1, d)
    mod = lambda mm, idx: mm[:, idx]

    bf = lambda w: w.astype(BF16)
    w1a, w1b, w1d = bf(w_ffn1_up[l][:, :f_dim]), bf(w_ffn1_up[l][:, f_dim:]), bf(w_ffn1_down[l])
    w2a, w2b, w2d = bf(w_ffn2_up[l][:, :f_dim]), bf(w_ffn2_up[l][:, f_dim:]), bf(w_ffn2_down[l])
    w_in_b = bf(w_in[l])

    tm = _tile(seq, 512)
    tmc = _tile(ctx.shape[1], 256)

    h = _ffn(x, mod(m, 0), mod(m, 1), mod(m, 2), norm_ffn1[l], w1a, w1b, w1d, tm=tm)
    hc = _ffn(ctx, mod(mc, 0), mod(mc, 1), mod(mc, 2), norm_ffn1[l], w1a, w1b, w1d, tm=tmc)

    qc, gfc, gbc, kfc, kbc, vc, _, _ = _proj(hc, mod(mc, 3), mod(mc, 4), norm_mix[l], lb, w_in_b, tm=tmc)
    q, gf, gb, kf, kb, v, gos, glu = _proj(h, mod(m, 3), mod(m, 4), norm_mix[l], lb, w_in_b, tm=tm)

    s0 = jnp.zeros((bsz, 2, HEADS, HEAD_DIM, HEAD_DIM), F32)
    _, _, s_ctx = _scan(qc, gfc, gbc, kfc, kbc, vc, s0, tb=_tile(ctx.shape[1], 256))
    o_f, o_b, _ = _scan(q, gf, gb, kf, kb, v, s_ctx, tb=_tile(seq, 256))

    h = _merge(h, mod(m, 3), mod(m, 4), mod(m, 5), o_f, o_b, gos, glu, norm_mix[l], hgrn_norm[l],
               w_dw[l], b_dw[l], conv_ln_g[l], conv_ln_b[l], bf(w_proj_a[l]), bf(w_proj_b[l]),
               bf(w_gate[l]), b_gate[l], bf(w_o[l]), tm=_tile(seq, 256))

    return _ffn(h, mod(m, 6), mod(m, 7), mod(m, 8), norm_ffn2[l], w2a, w2b, w2d, final_norm, tm=tm)
```

```python
import functools

import jax
import jax.numpy as jnp
from jax import lax
from jax.experimental import pallas as pl
from jax.experimental.pallas import tpu as pltpu

F32 = jnp.float32
BF16 = jnp.bfloat16

EPS = 1e-6
HEADS = 8
HEAD_DIM = 128
N_MOD = 9
CONV_WIDTH = 31
CONV_PAD = CONV_WIDTH // 2
HALO = 16
CHUNK = 64
SUB = 16
V7X_VMEM_LIMIT_BYTES = 56 * 1024 * 1024

NT_DIMS = (((1,), (1,)), ((), ()))
TN_DIMS = (((0,), (0,)), ((), ()))


def _params(*sem):
    return pltpu.CompilerParams(dimension_semantics=sem, vmem_limit_bytes=V7X_VMEM_LIMIT_BYTES)


def _const_spec(shape):
    nd = len(shape)
    return pl.BlockSpec(shape, lambda *_: (0,) * nd, pipeline_mode=pl.Buffered(1))


def _rms(x):
    return x * lax.rsqrt(jnp.mean(x * x, axis=-1, keepdims=True) + EPS)


def _dot(a, b):
    return jnp.dot(a, b, preferred_element_type=F32)


def _ada_kernel(c_ref, w_ref, b_ref, o_ref):
    a = c_ref[...]
    s = (a * jax.nn.sigmoid(a)).astype(BF16)
    o_ref[...] = _dot(s, w_ref[...].astype(BF16)) + b_ref[...]


def _ada(cc, w, b, *, tn=1024):
    rows, d = cc.shape
    n = w.shape[1]
    return pl.pallas_call(
        _ada_kernel,
        grid=(n // tn,),
        in_specs=[pl.BlockSpec((rows, d), lambda j: (0, 0)),
                  pl.BlockSpec((d, tn), lambda j: (0, j)),
                  pl.BlockSpec((1, tn), lambda j: (0, j))],
        out_specs=pl.BlockSpec((rows, tn), lambda j: (0, j)),
        out_shape=jax.ShapeDtypeStruct((rows, n), F32),
        compiler_params=_params("parallel"),
        name="ada",
    )(cc, w, b.reshape(1, n))


def _ffn_kernel(h_ref, sh_ref, sc_ref, gt_ref, gain_ref, wa_ref, wb_ref, wd_ref, fg_ref, o_ref,
                *, final):
    h = h_ref[0]
    u = (_rms(h) * gain_ref[...]) * (1.0 + sc_ref[0]) + sh_ref[0]
    ub = u.astype(BF16)
    a = _dot(ub, wa_ref[...])
    b = _dot(ub, wb_ref[...])
    act = (a * jax.nn.sigmoid(a) * b).astype(BF16)
    out = h + (0.5 * gt_ref[0]) * _dot(act, wd_ref[...])
    if final:
        out = _rms(out) * fg_ref[...]
    o_ref[0] = out


def _ffn(h, shift, scale, gate, gain, wa, wb, wd, final_gain=None, *, tm):
    bsz, t, d = h.shape
    f_dim = wa.shape[1]
    per_batch = shift.shape[0] == bsz
    mod_spec = pl.BlockSpec((1, 1, d), (lambda b, i: (b, 0, 0)) if per_batch else (lambda b, i: (0, 0, 0)))
    final = final_gain is not None
    fg = final_gain if final else gain
    tok_spec = pl.BlockSpec((1, tm, d), lambda b, i: (b, i, 0))
    return pl.pallas_call(
        functools.partial(_ffn_kernel, final=final),
        grid=(bsz, t // tm),
        in_specs=[tok_spec, mod_spec, mod_spec, mod_spec, _const_spec((1, d)),
                  _const_spec((d, f_dim)), _const_spec((d, f_dim)), _const_spec((f_dim, d)),
                  _const_spec((1, d))],
        out_specs=tok_spec,
        out_shape=jax.ShapeDtypeStruct((bsz, t, d), F32),
        compiler_params=_params("parallel", "parallel"),
        name="ffn_final" if final else "ffn",
    )(h, shift, scale, gate, gain.reshape(1, d), wa, wb, wd, fg.reshape(1, d))


def _chunk_cumsum(g, reverse):
    rows, n = g.shape
    x = g.reshape(rows // 8, 8, n)
    sub = lax.broadcasted_iota(jnp.int32, x.shape, 1)
    for s in (1, 2, 4):
        if reverse:
            x = x + jnp.where(sub < 8 - s, pltpu.roll(x, 8 - s, axis=1), 0.0)
        else:
            x = x + jnp.where(sub >= s, pltpu.roll(x, s, axis=1), 0.0)
    per = CHUNK // 8
    out = []
    for c in range(rows // CHUNK):
        blk = [None] * per
        run = None
        for j in (reversed(range(per)) if reverse else range(per)):
            grp = x[c * per + j]
            blk[j] = grp if run is None else grp + run
            tot = grp[0:1] if reverse else grp[7:8]
            run = tot if run is None else run + tot
        out.extend(blk)
    return jnp.concatenate(out, axis=0)


def _proj_kernel(h_ref, sh_ref, sc_ref, gain_ref, lb_ref, w_ref,
                 q_ref, gf_ref, gb_ref, kf_ref, kb_ref, v_ref, go_ref, glu_ref, p_ref):
    h = h_ref[0]
    d = h.shape[-1]
    u = ((_rms(h) * gain_ref[...]) * (1.0 + sc_ref[0]) + sh_ref[0]).astype(BF16)
    p_ref[...] = _dot(u, w_ref[...])

    def col(g):
        return p_ref[:, g * d:(g + 1) * d]

    q = col(0)
    q_ref[0] = (q * jax.nn.sigmoid(q)).astype(BF16)
    for direction, (g_ref, k_ref) in enumerate(((gf_ref, kf_ref), (gb_ref, kb_ref))):
        lb = lb_ref[direction:direction + 1, :]
        f = lb + (1.0 - lb) * jax.nn.sigmoid(col(1 + direction))
        k_ref[0] = (1.0 - f).astype(BF16)
        g_ref[0] = _chunk_cumsum(jnp.log(f), reverse=direction == 1)
    v_ref[0] = col(3).astype(BF16)
    go = col(4)
    go_ref[0] = (go * jax.nn.sigmoid(go)).astype(BF16)
    glu_ref[0] = col(5) * jax.nn.sigmoid(col(6))


def _proj(h, shift, scale, gain, lb, w_in, *, tm):
    bsz, t, d = h.shape
    per_batch = shift.shape[0] == bsz
    mod_spec = pl.BlockSpec((1, 1, d), (lambda b, i: (b, 0, 0)) if per_batch else (lambda b, i: (0, 0, 0)))
    tok_spec = pl.BlockSpec((1, tm, d), lambda b, i: (b, i, 0))
    sds = lambda dt: jax.ShapeDtypeStruct((bsz, t, d), dt)
    return pl.pallas_call(
        _proj_kernel,
        grid=(bsz, t // tm),
        in_specs=[tok_spec, mod_spec, mod_spec, _const_spec((1, d)), _const_spec((2, d)), _const_spec(w_in.shape)],
        out_specs=[tok_spec] * 8,
        out_shape=[sds(BF16), sds(F32), sds(F32), sds(BF16), sds(BF16), sds(BF16), sds(BF16), sds(F32)],
        scratch_shapes=[pltpu.VMEM((tm, w_in.shape[1]), F32)],
        compiler_params=_params("parallel", "parallel"),
        name="proj",
    )(h, shift, scale, gain.reshape(1, d), lb, w_in)


def _scan_local(direction, c, q_ref, g_ref, k_ref, v_ref, o_ref, qt_ref, u_ref, keep):
    fwd = direction == 0
    r0 = pl.multiple_of(c * CHUNK, CHUNK)
    rows = pl.ds(r0, CHUNK)
    G = g_ref[0, rows, :]
    k = k_ref[0, rows, :].astype(F32)
    q = q_ref[0, rows, :].astype(F32)
    v = v_ref[0, rows, :]
    gend = G[CHUNK - 1:CHUNK] if fwd else G[0:1]
    qt_ref[direction, rows, :] = (q * jnp.exp(G)).astype(BF16)
    kt = (k * jnp.exp(gend - G)).astype(BF16)

    n_sub = CHUNK // SUB
    qis, kis = [], []
    for i in range(n_sub):
        rs = slice(SUB * i, SUB * (i + 1))
        lo, hi = (0, SUB * (i + 1)) if fwd else (SUB * i, CHUNK)
        mid = SUB * i + (SUB // 2 - 1 if fwd else SUB // 2)
        rm = G[mid:mid + 1]
        qis.append((q[rs] * jnp.exp(G[rs] - rm)).astype(BF16))
        parts = [k[lo:hi] * jnp.exp(rm - G[lo:hi])]
        if lo > 0:
            parts.insert(0, jnp.zeros((lo, k.shape[1]), F32))
        if hi < CHUNK:
            parts.append(jnp.zeros((CHUNK - hi, k.shape[1]), F32))
        kis.append(jnp.concatenate(parts, axis=0).astype(BF16))

    heads = [slice(h * HEAD_DIM, (h + 1) * HEAD_DIM) for h in range(HEADS)]
    a = [[lax.dot_general(qis[i][:, hl], kis[i][:, hl], NT_DIMS, preferred_element_type=F32)
          for i in range(n_sub)] for hl in heads]
    for h, hl in enumerate(heads):
        u_ref[direction, c, h] = lax.dot_general(v[:, hl], kt[:, hl], TN_DIMS, preferred_element_type=F32)
    am = [jnp.where(keep, jnp.concatenate(a[h], axis=0), 0.0).astype(BF16) for h in range(HEADS)]
    for h, hl in enumerate(heads):
        o_ref[0, rows, hl] = _dot(am[h], v[:, hl])


def _scan_carry(direction, c, n_chunks, g_ref, o_ref, qt_ref, u_ref, st_ref):
    fwd = direction == 0
    cc = c if fwd else n_chunks - 1 - c
    r0 = cc * CHUNK
    rows = pl.ds(r0, CHUNK)
    dec = jnp.exp(g_ref[0, pl.ds(r0 + (CHUNK - 1 if fwd else 0), 1), :])
    for h in range(HEADS):
        hl = slice(h * HEAD_DIM, (h + 1) * HEAD_DIM)
        st = st_ref[direction, h]
        o_ref[0, rows, hl] += lax.dot_general(qt_ref[direction, rows, hl], st.astype(BF16), NT_DIMS,
                                              preferred_element_type=F32)
        st_ref[direction, h] = st * dec[:, hl] + u_ref[direction, cc, h]


def _scan_kernel(qf_ref, gf_ref, kf_ref, vf_ref, qb_ref, gb_ref, kb_ref, vb_ref, s0_ref,
                 of_ref, ob_ref, sfin_ref, st_ref, qt_ref, u_ref, *, n_chunks):
    j = pl.program_id(1)

    @pl.when(j == 0)
    def _():
        st_ref[...] = s0_ref[0]

    ri = lax.broadcasted_iota(jnp.int32, (CHUNK, CHUNK), 0)
    ci = lax.broadcasted_iota(jnp.int32, (CHUNK, CHUNK), 1)
    keep_f, keep_b = ci <= ri, ci >= ri

    def local(c, carry):
        _scan_local(0, c, qf_ref, gf_ref, kf_ref, vf_ref, of_ref, qt_ref, u_ref, keep_f)
        _scan_local(1, c, qb_ref, gb_ref, kb_ref, vb_ref, ob_ref, qt_ref, u_ref, keep_b)
        return carry

    lax.fori_loop(0, n_chunks, local, 0)

    for c in range(n_chunks):
        _scan_carry(0, c, n_chunks, gf_ref, of_ref, qt_ref, u_ref, st_ref)
        _scan_carry(1, c, n_chunks, gb_ref, ob_ref, qt_ref, u_ref, st_ref)

    @pl.when(j == pl.num_programs(1) - 1)
    def _():
        sfin_ref[0] = st_ref[...]


def _scan(q, gf, gb, kf, kb, v, s0, *, tb):
    bsz, t, d = q.shape
    nb = t // tb
    n_chunks = tb // CHUNK
    fwd_spec = pl.BlockSpec((1, tb, d), lambda b, j: (b, j, 0))
    bwd_spec = pl.BlockSpec((1, tb, d), lambda b, j: (b, nb - 1 - j, 0))
    st_shape = (2, HEADS, HEAD_DIM, HEAD_DIM)
    st_spec = pl.BlockSpec((1,) + st_shape, lambda b, j: (b, 0, 0, 0, 0))
    return pl.pallas_call(
        functools.partial(_scan_kernel, n_chunks=n_chunks),
        grid=(bsz, nb),
        in_specs=[fwd_spec] * 4 + [bwd_spec] * 4 + [st_spec],
        out_specs=[fwd_spec, bwd_spec, st_spec],
        out_shape=[jax.ShapeDtypeStruct((bsz, t, d), F32), jax.ShapeDtypeStruct((bsz, t, d), F32),
                   jax.ShapeDtypeStruct((bsz,) + st_shape, F32)],
        scratch_shapes=[pltpu.VMEM(st_shape, F32), pltpu.VMEM((2, tb, d), BF16),
                        pltpu.VMEM((2, n_chunks, HEADS, HEAD_DIM, HEAD_DIM), F32)],
        compiler_params=_params("parallel", "arbitrary"),
        name="scan",
    )(q, gf, kf, v, q, gb, kb, v, s0)


def _merge_kernel(h_ref, sh_ref, sc_ref, gt_ref, of_ref, ob_ref, go_ref, glu_ref, prev_ref, next_ref,
                  gain_ref, hn_ref, wdw_ref, bdw_ref, lng_ref, lnb_ref, wpa_ref, wpb_ref,
                  wg_ref, bg_ref, wo_ref, o_ref, xa_ref, xp_ref, *, tm):
    i = pl.program_id(1)
    h = h_ref[0]
    d = h.shape[-1]
    u = ((_rms(h) * gain_ref[...]) * (1.0 + sc_ref[0]) + sh_ref[0]).astype(BF16)
    gates = jax.nn.sigmoid(_dot(u, wg_ref[...]) + bg_ref[...])

    o = of_ref[0] + ob_ref[0]
    for hd in range(HEADS):
        hl = slice(hd * HEAD_DIM, (hd + 1) * HEAD_DIM)
        oh = _rms(o[:, hl]) * hn_ref[...]
        xa_ref[:, hl] = (oh.astype(F32) * go_ref[0, :, hl].astype(F32)).astype(BF16)
    y_a = _dot(xa_ref[...], wpa_ref[...])

    xp_ref[0:HALO, :] = jnp.where(i > 0, prev_ref[0], 0.0)
    xp_ref[HALO:HALO + tm, :] = glu_ref[0]
    xp_ref[HALO + tm:2 * HALO + tm, :] = jnp.where(i < pl.num_programs(1) - 1, next_ref[0], 0.0)
    hb = bdw_ref[...]
    for r in range(8):
        offs = [o for o in range(HALO - CONV_PAD, HALO - CONV_PAD + CONV_WIDTH) if o % 8 == r]
        span = max(offs) - r + tm
        shifted = xp_ref[r:r + span, :]
        acc = None
        for off in offs:
            j = off - (HALO - CONV_PAD)
            term = shifted[off - r:off - r + tm] * wdw_ref[j:j + 1, :]
            acc = term if acc is None else acc + term
        hb = hb + acc
    mu = jnp.mean(hb, axis=-1, keepdims=True)
    cen = hb - mu
    var = jnp.mean(cen * cen, axis=-1, keepdims=True)
    ln = cen * lax.rsqrt(var + EPS) * lng_ref[...] + lnb_ref[...]
    y_b = _dot((ln * jax.nn.sigmoid(ln)).astype(BF16), wpb_ref[...])

    mixed = (gates[:, :d] * y_a + gates[:, d:] * y_b).astype(BF16)
    o_ref[0] = h + gt_ref[0] * _dot(mixed, wo_ref[...])


def _merge(h, shift, scale, gate, o_f, o_b, gos, glu, gain, hgrn_norm, w_dw, b_dw, ln_g, ln_b,
           w_pa, w_pb, w_gate, b_gate, w_o, *, tm):
    bsz, t, d = h.shape
    nt = t // tm
    per_tile = tm // HALO
    n_halo = t // HALO
    tok_spec = pl.BlockSpec((1, tm, d), lambda b, i: (b, i, 0))
    mod_spec = pl.BlockSpec((1, 1, d), lambda b, i: (b, 0, 0))
    prev_spec = pl.BlockSpec((1, HALO, d), lambda b, i: (b, jnp.maximum(i * per_tile - 1, 0), 0))
    next_spec = pl.BlockSpec((1, HALO, d), lambda b, i: (b, jnp.minimum((i + 1) * per_tile, n_halo - 1), 0))
    row = lambda a: a.reshape(1, -1)
    return pl.pallas_call(
        functools.partial(_merge_kernel, tm=tm),
        grid=(bsz, nt),
        in_specs=[tok_spec, mod_spec, mod_spec, mod_spec, tok_spec, tok_spec, tok_spec, tok_spec,
                  prev_spec, next_spec,
                  _const_spec((1, d)), _const_spec((1, HEAD_DIM)), _const_spec((CONV_WIDTH, d)),
                  _const_spec((1, d)), _const_spec((1, d)), _const_spec((1, d)),
                  _const_spec((d, d)), _const_spec((d, d)), _const_spec((d, 2 * d)),
                  _const_spec((1, 2 * d)), _const_spec((d, d))],
        out_specs=tok_spec,
        out_shape=jax.ShapeDtypeStruct((bsz, t, d), F32),
        scratch_shapes=[pltpu.VMEM((tm, d), BF16), pltpu.VMEM((tm + 2 * HALO, d), F32)],
        compiler_params=_params("parallel", "parallel"),
        name="merge",
    )(h, shift, scale, gate, o_f, o_b, gos, glu, glu, glu,
      row(gain), row(hgrn_norm), w_dw, row(b_dw), row(ln_g), row(ln_b), w_pa, w_pb, w_gate, row(b_gate), w_o)


def _tile(t, want):
    tm = min(t, want)
    assert t % tm == 0
    return tm


def kernel(x, c, ctx, c_ctx, w_ada, b_ada, norm_ffn1, w_ffn1_up, w_ffn1_down, norm_mix, w_in, lb_logits,
           hgrn_norm, w_dw, b_dw, conv_ln_g, conv_ln_b, w_proj_a, w_proj_b, w_gate, b_gate, w_o, norm_ffn2,
           w_ffn2_up, w_ffn2_down, final_norm):
    bsz, seq, d = x.shape
    depth = w_ada.shape[0]
    assert depth == 1 and d == HEADS * HEAD_DIM
    f_dim = w_ffn1_down.shape[1]
    l = 0

    lb = jnp.cumsum(jax.nn.softmax(lb_logits.astype(F32), axis=0), axis=0)[l]

    n_rows = -(-(bsz + 1) // 8) * 8
    cc = jnp.zeros((n_rows, d), F32).at[:bsz].set(c).at[bsz].set(c_ctx)
    mods = _ada(cc, w_ada[l], b_ada[l])
    m = mods[:bsz].reshape(bsz, N_MOD, 1, d)
    mc = mods[bsz:bsz + 1].reshape(1, N_MOD, 1, d)
    mod = lambda mm, idx: mm[:, idx]

    bf = lambda w: w.astype(BF16)
    w1a, w1b, w1d = bf(w_ffn1_up[l][:, :f_dim]), bf(w_ffn1_up[l][:, f_dim:]), bf(w_ffn1_down[l])
    w2a, w2b, w2d = bf(w_ffn2_up[l][:, :f_dim]), bf(w_ffn2_up[l][:, f_dim:]), bf(w_ffn2_down[l])
    w_in_b = bf(w_in[l])

    tm = _tile(seq, 512)
    tmc = _tile(ctx.shape[1], 256)

    h = _ffn(x, mod(m, 0), mod(m, 1), mod(m, 2), norm_ffn1[l], w1a, w1b, w1d, tm=tm)
    hc = _ffn(ctx, mod(mc, 0), mod(mc, 1), mod(mc, 2), norm_ffn1[l], w1a, w1b, w1d, tm=tmc)

    qc, gfc, gbc, kfc, kbc, vc, _, _ = _proj(hc, mod(mc, 3), mod(mc, 4), norm_mix[l], lb, w_in_b, tm=tmc)
    q, gf, gb, kf, kb, v, gos, glu = _proj(h, mod(m, 3), mod(m, 4), norm_mix[l], lb, w_in_b, tm=_tile(seq, 256))

    s0 = jnp.zeros((bsz, 2, HEADS, HEAD_DIM, HEAD_DIM), F32)
    _, _, s_ctx = _scan(qc, gfc, gbc, kfc, kbc, vc, s0, tb=_tile(ctx.shape[1], 256))
    o_f, o_b, _ = _scan(q, gf, gb, kf, kb, v, s_ctx, tb=_tile(seq, 256))

    h = _merge(h, mod(m, 3), mod(m, 4), mod(m, 5), o_f, o_b, gos, glu, norm_mix[l], hgrn_norm[l],
               w_dw[l], b_dw[l], conv_ln_g[l], conv_ln_b[l], bf(w_proj_a[l]), bf(w_proj_b[l]),
               bf(w_gate[l]), b_gate[l], bf(w_o[l]), tm=_tile(seq, 256))

    return _ffn(h, mod(m, 6), mod(m, 7), mod(m, 8), norm_ffn2[l], w2a, w2b, w2d, final_norm, tm=tm)
```

```python
import functools

import jax
import jax.numpy as jnp
from jax import lax
from jax.experimental import pallas as pl
from jax.experimental.pallas import tpu as pltpu

F32 = jnp.float32
BF16 = jnp.bfloat16

EPS = 1e-6
HEADS = 8
HEAD_DIM = 128
N_MOD = 9
CONV_WIDTH = 31
CONV_PAD = CONV_WIDTH // 2
HALO = 16
CONV_ROWS = 64
CHUNK = 64
SUB = 16
V7X_VMEM_LIMIT_BYTES = 56 * 1024 * 1024

NT_DIMS = (((1,), (1,)), ((), ()))
TN_DIMS = (((0,), (0,)), ((), ()))


def _params(*sem):
    return pltpu.CompilerParams(dimension_semantics=sem, vmem_limit_bytes=V7X_VMEM_LIMIT_BYTES)


def _const_spec(shape):
    nd = len(shape)
    return pl.BlockSpec(shape, lambda *_: (0,) * nd, pipeline_mode=pl.Buffered(1))


def _rms(x):
    return x * lax.rsqrt(jnp.mean(x * x, axis=-1, keepdims=True) + EPS)


def _dot(a, b):
    return jnp.dot(a, b, preferred_element_type=F32)


def _modulated(h, gain, shift, scale):
    return ((_rms(h) * gain) * (1.0 + scale) + shift).astype(BF16)


def _swiglu(u, wa_ref, wb_ref, wd_ref):
    a = _dot(u, wa_ref[...])
    b = _dot(u, wb_ref[...])
    return _dot((a * jax.nn.sigmoid(a) * b).astype(BF16), wd_ref[...])


def _ada_kernel(c_ref, w_ref, b_ref, o_ref):
    a = c_ref[...]
    s = (a * jax.nn.sigmoid(a)).astype(BF16)
    o_ref[...] = _dot(s, w_ref[...].astype(BF16)) + b_ref[...]


def _ada(cc, w, b, *, tn=1024):
    rows, d = cc.shape
    n = w.shape[1]
    return pl.pallas_call(
        _ada_kernel,
        grid=(n // tn,),
        in_specs=[pl.BlockSpec((rows, d), lambda j: (0, 0)),
                  pl.BlockSpec((d, tn), lambda j: (0, j)),
                  pl.BlockSpec((1, tn), lambda j: (0, j))],
        out_specs=pl.BlockSpec((rows, tn), lambda j: (0, j)),
        out_shape=jax.ShapeDtypeStruct((rows, n), F32),
        compiler_params=_params("parallel"),
        name="ada",
    )(cc, w, b.reshape(1, n))


def _ffn_kernel(h_ref, sh_ref, sc_ref, gt_ref, gain_ref, wa_ref, wb_ref, wd_ref, o_ref):
    h = h_ref[0]
    u = _modulated(h, gain_ref[...], sh_ref[0], sc_ref[0])
    o_ref[0] = h + (0.5 * gt_ref[0]) * _swiglu(u, wa_ref, wb_ref, wd_ref)


def _ffn(h, shift, scale, gate, gain, wa, wb, wd, *, tm):
    bsz, t, d = h.shape
    f_dim = wa.shape[1]
    per_batch = shift.shape[0] == bsz
    mod_spec = pl.BlockSpec((1, 1, d), (lambda b, i: (b, 0, 0)) if per_batch else (lambda b, i: (0, 0, 0)))
    tok_spec = pl.BlockSpec((1, tm, d), lambda b, i: (b, i, 0))
    return pl.pallas_call(
        _ffn_kernel,
        grid=(bsz, t // tm),
        in_specs=[tok_spec, mod_spec, mod_spec, mod_spec, _const_spec((1, d)),
                  _const_spec((d, f_dim)), _const_spec((d, f_dim)), _const_spec((f_dim, d))],
        out_specs=tok_spec,
        out_shape=jax.ShapeDtypeStruct((bsz, t, d), F32),
        compiler_params=_params("parallel", "parallel"),
        name="ffn",
    )(h, shift, scale, gate, gain.reshape(1, d), wa, wb, wd)


def _chunk_cumsum(g, reverse):
    rows, n = g.shape
    x = g.reshape(rows // 8, 8, n)
    sub = lax.broadcasted_iota(jnp.int32, x.shape, 1)
    for s in (1, 2, 4):
        if reverse:
            x = x + jnp.where(sub < 8 - s, pltpu.roll(x, 8 - s, axis=1), 0.0)
        else:
            x = x + jnp.where(sub >= s, pltpu.roll(x, s, axis=1), 0.0)
    per = CHUNK // 8
    out = []
    for c in range(rows // CHUNK):
        blk = [None] * per
        run = None
        for j in (reversed(range(per)) if reverse else range(per)):
            grp = x[c * per + j]
            blk[j] = grp if run is None else grp + run
            tot = grp[0:1] if reverse else grp[7:8]
            run = tot if run is None else run + tot
        out.extend(blk)
    return jnp.concatenate(out, axis=0)


def _proj_kernel(h_ref, sh_ref, sc_ref, gain_ref, lb_ref, w_ref,
                 q_ref, gf_ref, gb_ref, kf_ref, kb_ref, v_ref, go_ref, glu_ref, p_ref):
    h = h_ref[0]
    d = h.shape[-1]
    u = _modulated(h, gain_ref[...], sh_ref[0], sc_ref[0])
    p_ref[...] = _dot(u, w_ref[...])

    def col(g):
        return p_ref[:, g * d:(g + 1) * d]

    q = col(0)
    q_ref[0] = (q * jax.nn.sigmoid(q)).astype(BF16)
    for direction, (g_ref, k_ref) in enumerate(((gf_ref, kf_ref), (gb_ref, kb_ref))):
        lb = lb_ref[direction:direction + 1, :]
        f = lb + (1.0 - lb) * jax.nn.sigmoid(col(1 + direction))
        k_ref[0] = (1.0 - f).astype(BF16)
        g_ref[0] = _chunk_cumsum(jnp.log(f), reverse=direction == 1)
    v_ref[0] = col(3).astype(BF16)
    go = col(4)
    go_ref[0] = (go * jax.nn.sigmoid(go)).astype(BF16)
    glu_ref[0] = col(5) * jax.nn.sigmoid(col(6))


def _proj(h, shift, scale, gain, lb, w_in, *, tm):
    bsz, t, d = h.shape
    per_batch = shift.shape[0] == bsz
    mod_spec = pl.BlockSpec((1, 1, d), (lambda b, i: (b, 0, 0)) if per_batch else (lambda b, i: (0, 0, 0)))
    tok_spec = pl.BlockSpec((1, tm, d), lambda b, i: (b, i, 0))
    sds = lambda dt: jax.ShapeDtypeStruct((bsz, t, d), dt)
    return pl.pallas_call(
        _proj_kernel,
        grid=(bsz, t // tm),
        in_specs=[tok_spec, mod_spec, mod_spec, _const_spec((1, d)), _const_spec((2, d)), _const_spec(w_in.shape)],
        out_specs=[tok_spec] * 8,
        out_shape=[sds(BF16), sds(F32), sds(F32), sds(BF16), sds(BF16), sds(BF16), sds(BF16), sds(F32)],
        scratch_shapes=[pltpu.VMEM((tm, w_in.shape[1]), F32)],
        compiler_params=_params("parallel", "parallel"),
        name="proj",
    )(h, shift, scale, gain.reshape(1, d), lb, w_in)


def _scan_local(direction, c, q_ref, g_ref, k_ref, v_ref, o_ref, qt_ref, u_ref, keep):
    fwd = direction == 0
    r0 = pl.multiple_of(c * CHUNK, CHUNK)
    rows = pl.ds(r0, CHUNK)
    G = g_ref[0, rows, :]
    k = k_ref[0, rows, :].astype(F32)
    q = q_ref[0, rows, :].astype(F32)
    v = v_ref[0, rows, :]
    gend = G[CHUNK - 1:CHUNK] if fwd else G[0:1]
    qt_ref[direction, rows, :] = (q * jnp.exp(G)).astype(BF16)
    kt = (k * jnp.exp(gend - G)).astype(BF16)

    n_sub = CHUNK // SUB
    qis, kis = [], []
    for i in range(n_sub):
        rs = slice(SUB * i, SUB * (i + 1))
        lo, hi = (0, SUB * (i + 1)) if fwd else (SUB * i, CHUNK)
        mid = SUB * i + (SUB // 2 - 1 if fwd else SUB // 2)
        rm = G[mid:mid + 1]
        qis.append((q[rs] * jnp.exp(G[rs] - rm)).astype(BF16))
        parts = [k[lo:hi] * jnp.exp(rm - G[lo:hi])]
        if lo > 0:
            parts.insert(0, jnp.zeros((lo, k.shape[1]), F32))
        if hi < CHUNK:
            parts.append(jnp.zeros((CHUNK - hi, k.shape[1]), F32))
        kis.append(jnp.concatenate(parts, axis=0).astype(BF16))

    heads = [slice(h * HEAD_DIM, (h + 1) * HEAD_DIM) for h in range(HEADS)]
    a = [[lax.dot_general(qis[i][:, hl], kis[i][:, hl], NT_DIMS, preferred_element_type=F32)
          for i in range(n_sub)] for hl in heads]
    for h, hl in enumerate(heads):
        u_ref[direction, c, h] = lax.dot_general(v[:, hl], kt[:, hl], TN_DIMS, preferred_element_type=F32)
    am = [jnp.where(keep, jnp.concatenate(a[h], axis=0), 0.0).astype(BF16) for h in range(HEADS)]
    for h, hl in enumerate(heads):
        o_ref[0, rows, hl] = _dot(am[h], v[:, hl])


def _scan_carry(direction, c, n_chunks, g_ref, o_ref, qt_ref, u_ref, st_ref):
    fwd = direction == 0
    cc = c if fwd else n_chunks - 1 - c
    r0 = cc * CHUNK
    rows = pl.ds(r0, CHUNK)
    dec = jnp.exp(g_ref[0, pl.ds(r0 + (CHUNK - 1 if fwd else 0), 1), :])
    for h in range(HEADS):
        hl = slice(h * HEAD_DIM, (h + 1) * HEAD_DIM)
        st = st_ref[direction, h]
        o_ref[0, rows, hl] += lax.dot_general(qt_ref[direction, rows, hl], st.astype(BF16), NT_DIMS,
                                              preferred_element_type=F32)
        st_ref[direction, h] = st * dec[:, hl] + u_ref[direction, cc, h]


def _scan_kernel(qf_ref, gf_ref, kf_ref, vf_ref, qb_ref, gb_ref, kb_ref, vb_ref, s0_ref,
                 of_ref, ob_ref, sfin_ref, st_ref, qt_ref, u_ref, *, n_chunks):
    j = pl.program_id(1)

    @pl.when(j == 0)
    def _():
        st_ref[...] = s0_ref[0]

    ri = lax.broadcasted_iota(jnp.int32, (CHUNK, CHUNK), 0)
    ci = lax.broadcasted_iota(jnp.int32, (CHUNK, CHUNK), 1)
    keep_f, keep_b = ci <= ri, ci >= ri

    def local(c, carry):
        _scan_local(0, c, qf_ref, gf_ref, kf_ref, vf_ref, of_ref, qt_ref, u_ref, keep_f)
        _scan_local(1, c, qb_ref, gb_ref, kb_ref, vb_ref, ob_ref, qt_ref, u_ref, keep_b)
        return carry

    lax.fori_loop(0, n_chunks, local, 0, unroll=True)

    for c in range(n_chunks):
        _scan_carry(0, c, n_chunks, gf_ref, of_ref, qt_ref, u_ref, st_ref)
        _scan_carry(1, c, n_chunks, gb_ref, ob_ref, qt_ref, u_ref, st_ref)

    @pl.when(j == pl.num_programs(1) - 1)
    def _():
        sfin_ref[0] = st_ref[...]


def _scan(q, gf, gb, kf, kb, v, s0, *, tb):
    bsz, t, d = q.shape
    nb = t // tb
    n_chunks = tb // CHUNK
    fwd_spec = pl.BlockSpec((1, tb, d), lambda b, j: (b, j, 0))
    bwd_spec = pl.BlockSpec((1, tb, d), lambda b, j: (b, nb - 1 - j, 0))
    st_shape = (2, HEADS, HEAD_DIM, HEAD_DIM)
    st_spec = pl.BlockSpec((1,) + st_shape, lambda b, j: (b, 0, 0, 0, 0))
    return pl.pallas_call(
        functools.partial(_scan_kernel, n_chunks=n_chunks),
        grid=(bsz, nb),
        in_specs=[fwd_spec] * 4 + [bwd_spec] * 4 + [st_spec],
        out_specs=[fwd_spec, bwd_spec, st_spec],
        out_shape=[jax.ShapeDtypeStruct((bsz, t, d), F32), jax.ShapeDtypeStruct((bsz, t, d), F32),
                   jax.ShapeDtypeStruct((bsz,) + st_shape, F32)],
        scratch_shapes=[pltpu.VMEM(st_shape, F32), pltpu.VMEM((2, tb, d), BF16),
                        pltpu.VMEM((2, n_chunks, HEADS, HEAD_DIM, HEAD_DIM), F32)],
        compiler_params=_params("parallel", "arbitrary"),
        name="scan",
    )(q, gf, kf, v, q, gb, kb, v, s0)


def _conv_branch(xp_ref, tm, wdw_ref, bdw_ref, lng_ref, lnb_ref):
    d = xp_ref.shape[1]
    out = []
    for rb in range(0, tm, CONV_ROWS):
        base = rb + HALO - CONV_PAD
        cols = []
        for c in range(0, d, 128):
            cl = slice(c, c + 128)
            hb = bdw_ref[:, cl]
            for r in range(8):
                offs = [o for o in range(base, base + CONV_WIDTH) if o % 8 == r]
                lo = min(offs)
                shifted = xp_ref[lo:max(offs) + CONV_ROWS, cl]
                acc = None
                for off in offs:
                    term = shifted[off - lo:off - lo + CONV_ROWS] * wdw_ref[off - base:off - base + 1, cl]
                    acc = term if acc is None else acc + term
                hb = hb + acc
            cols.append(hb)
        hb = jnp.concatenate(cols, axis=1)
        mu = jnp.mean(hb, axis=-1, keepdims=True)
        cen = hb - mu
        var = jnp.mean(cen * cen, axis=-1, keepdims=True)
        ln = cen * lax.rsqrt(var + EPS) * lng_ref[...] + lnb_ref[...]
        out.append((ln * jax.nn.sigmoid(ln)).astype(BF16))
    return jnp.concatenate(out, axis=0)


def _mixffn_kernel(h_ref, sh_ref, sc_ref, gt_ref, sh2_ref, sc2_ref, gt2_ref, of_ref, ob_ref, go_ref,
                   glu_ref, prev_ref, next_ref,
                   gain_ref, hn_ref, wdw_ref, bdw_ref, lng_ref, lnb_ref, wpa_ref, wpb_ref, wg_ref, bg_ref, wo_ref,
                   gain2_ref, wa_ref, wb_ref, wd_ref, fg_ref,
                   o_ref, xa_ref, xp_ref):
    i = pl.program_id(1)
    h = h_ref[0]
    tm, d = h.shape

    xp_ref[0:HALO, :] = jnp.where(i > 0, prev_ref[0], 0.0)
    xp_ref[HALO:HALO + tm, :] = glu_ref[0]
    xp_ref[HALO + tm:2 * HALO + tm, :] = jnp.where(i < pl.num_programs(1) - 1, next_ref[0], 0.0)
    act_b = _conv_branch(xp_ref, tm, wdw_ref, bdw_ref, lng_ref, lnb_ref)

    u = _modulated(h, gain_ref[...], sh_ref[0], sc_ref[0])
    gates = jax.nn.sigmoid(_dot(u, wg_ref[...]) + bg_ref[...])

    o = of_ref[0] + ob_ref[0]
    for hd in range(HEADS):
        hl = slice(hd * HEAD_DIM, (hd + 1) * HEAD_DIM)
        oh = _rms(o[:, hl]) * hn_ref[...]
        xa_ref[:, hl] = (oh * go_ref[0, :, hl].astype(F32)).astype(BF16)
    y_a = _dot(xa_ref[...], wpa_ref[...])
    y_b = _dot(act_b, wpb_ref[...])

    mixed = (gates[:, :d] * y_a + gates[:, d:] * y_b).astype(BF16)
    h = h + gt_ref[0] * _dot(mixed, wo_ref[...])

    u2 = _modulated(h, gain2_ref[...], sh2_ref[0], sc2_ref[0])
    h = h + (0.5 * gt2_ref[0]) * _swiglu(u2, wa_ref, wb_ref, wd_ref)
    o_ref[0] = _rms(h) * fg_ref[...]


def _mixffn(h, shift, scale, gate, shift2, scale2, gate2, o_f, o_b, gos, glu, gain, hgrn_norm, w_dw, b_dw,
            ln_g, ln_b, w_pa, w_pb, w_gate, b_gate, w_o, gain2, wa, wb, wd, final_gain, *, tm):
    bsz, t, d = h.shape
    f_dim = wa.shape[1]
    per_tile = tm // HALO
    n_halo = t // HALO
    tok_spec = pl.BlockSpec((1, tm, d), lambda b, i: (b, i, 0))
    mod_spec = pl.BlockSpec((1, 1, d), lambda b, i: (b, 0, 0))
    prev_spec = pl.BlockSpec((1, HALO, d), lambda b, i: (b, jnp.maximum(i * per_tile - 1, 0), 0))
    next_spec = pl.BlockSpec((1, HALO, d), lambda b, i: (b, jnp.minimum((i + 1) * per_tile, n_halo - 1), 0))
    row = lambda a: a.reshape(1, -1)
    return pl.pallas_call(
        _mixffn_kernel,
        grid=(bsz, t // tm),
        in_specs=[tok_spec] + [mod_spec] * 6 + [tok_spec, tok_spec, tok_spec, tok_spec, prev_spec, next_spec,
                  _const_spec((1, d)), _const_spec((1, HEAD_DIM)), _const_spec((CONV_WIDTH, d)),
                  _const_spec((1, d)), _const_spec((1, d)), _const_spec((1, d)),
                  _const_spec((d, d)), _const_spec((d, d)), _const_spec((d, 2 * d)),
                  _const_spec((1, 2 * d)), _const_spec((d, d)),
                  _const_spec((1, d)), _const_spec((d, f_dim)), _const_spec((d, f_dim)), _const_spec((f_dim, d)),
                  _const_spec((1, d))],
        out_specs=tok_spec,
        out_shape=jax.ShapeDtypeStruct((bsz, t, d), F32),
        scratch_shapes=[pltpu.VMEM((tm, d), BF16), pltpu.VMEM((tm + 2 * HALO, d), F32)],
        compiler_params=_params("parallel", "parallel"),
        name="mixffn",
    )(h, shift, scale, gate, shift2, scale2, gate2, o_f, o_b, gos, glu, glu, glu,
      row(gain), row(hgrn_norm), w_dw, row(b_dw), row(ln_g), row(ln_b), w_pa, w_pb, w_gate, row(b_gate), w_o,
      row(gain2), wa, wb, wd, row(final_gain))


def _tile(t, want):
    tm = min(t, want)
    assert t % tm == 0
    return tm


def kernel(x, c, ctx, c_ctx, w_ada, b_ada, norm_ffn1, w_ffn1_up, w_ffn1_down, norm_mix, w_in, lb_logits,
           hgrn_norm, w_dw, b_dw, conv_ln_g, conv_ln_b, w_proj_a, w_proj_b, w_gate, b_gate, w_o, norm_ffn2,
           w_ffn2_up, w_ffn2_down, final_norm):
    bsz, seq, d = x.shape
    depth = w_ada.shape[0]
    assert depth == 1 and d == HEADS * HEAD_DIM
    f_dim = w_ffn1_down.shape[1]
    l = 0

    lb = jnp.cumsum(jax.nn.softmax(lb_logits.astype(F32), axis=0), axis=0)[l]

    n_rows = -(-(bsz + 1) // 8) * 8
    cc = jnp.zeros((n_rows, d), F32).at[:bsz].set(c).at[bsz].set(c_ctx)
    mods = _ada(cc, w_ada[l], b_ada[l])
    m = mods[:bsz].reshape(bsz, N_MOD, 1, d)
    mc = mods[bsz:bsz + 1].reshape(1, N_MOD, 1, d)
    mod = lambda mm, idx: mm[:, idx]

    bf = lambda w: w.astype(BF16)
    w1a, w1b, w1d = bf(w_ffn1_up[l][:, :f_dim]), bf(w_ffn1_up[l][:, f_dim:]), bf(w_ffn1_down[l])
    w2a, w2b, w2d = bf(w_ffn2_up[l][:, :f_dim]), bf(w_ffn2_up[l][:, f_dim:]), bf(w_ffn2_down[l])
    w_in_b = bf(w_in[l])

    tmc = _tile(ctx.shape[1], 256)

    h = _ffn(x, mod(m, 0), mod(m, 1), mod(m, 2), norm_ffn1[l], w1a, w1b, w1d, tm=_tile(seq, 512))
    hc = _ffn(ctx, mod(mc, 0), mod(mc, 1), mod(mc, 2), norm_ffn1[l], w1a, w1b, w1d, tm=tmc)

    qc, gfc, gbc, kfc, kbc, vc, _, _ = _proj(hc, mod(mc, 3), mod(mc, 4), norm_mix[l], lb, w_in_b, tm=tmc)
    q, gf, gb, kf, kb, v, gos, glu = _proj(h, mod(m, 3), mod(m, 4), norm_mix[l], lb, w_in_b, tm=_tile(seq, 256))

    s0 = jnp.zeros((bsz, 2, HEADS, HEAD_DIM, HEAD_DIM), F32)
    _, _, s_ctx = _scan(qc, gfc, gbc, kfc, kbc, vc, s0, tb=tmc)
    o_f, o_b, _ = _scan(q, gf, gb, kf, kb, v, s_ctx, tb=_tile(seq, 512))

    return _mixffn(h, mod(m, 3), mod(m, 4), mod(m, 5), mod(m, 6), mod(m, 7), mod(m, 8), o_f, o_b, gos, glu,
                   norm_mix[l], hgrn_norm[l], w_dw[l], b_dw[l], conv_ln_g[l], conv_ln_b[l],
                   bf(w_proj_a[l]), bf(w_proj_b[l]), bf(w_gate[l]), b_gate[l], bf(w_o[l]),
                   norm_ffn2[l], w2a, w2b, w2d, final_norm, tm=_tile(seq, 256))
```

```python
import functools

import jax
import jax.numpy as jnp
from jax import lax
from jax.experimental import pallas as pl
from jax.experimental.pallas import tpu as pltpu

F32 = jnp.float32
BF16 = jnp.bfloat16

EPS = 1e-6
HEADS = 8
HEAD_DIM = 128
N_MOD = 9
CONV_WIDTH = 31
CONV_PAD = CONV_WIDTH // 2
HALO = 16
CONV_ROWS = 64
CHUNK = 64
SUB = 16
V7X_VMEM_LIMIT_BYTES = 56 * 1024 * 1024

NT_DIMS = (((1,), (1,)), ((), ()))
TN_DIMS = (((0,), (0,)), ((), ()))


def _params(*sem):
    return pltpu.CompilerParams(dimension_semantics=sem, vmem_limit_bytes=V7X_VMEM_LIMIT_BYTES)


def _const_spec(shape):
    nd = len(shape)
    return pl.BlockSpec(shape, lambda *_: (0,) * nd, pipeline_mode=pl.Buffered(1))


def _rms(x):
    return x * lax.rsqrt(jnp.mean(x * x, axis=-1, keepdims=True) + EPS)


def _dot(a, b):
    return jnp.dot(a, b, preferred_element_type=F32)


def _modulated(h, gain, shift, scale):
    return ((_rms(h) * gain) * (1.0 + scale) + shift).astype(BF16)


def _swiglu(u, wa_ref, wb_ref, wd_ref):
    a = _dot(u, wa_ref[...])
    b = _dot(u, wb_ref[...])
    return _dot((a * jax.nn.sigmoid(a) * b).astype(BF16), wd_ref[...])


def _ada_kernel(c_ref, w_ref, b_ref, o_ref):
    a = c_ref[...]
    s = (a * jax.nn.sigmoid(a)).astype(BF16)
    o_ref[...] = _dot(s, w_ref[...].astype(BF16)) + b_ref[...]


def _ada(cc, w, b, *, tn=1024):
    rows, d = cc.shape
    n = w.shape[1]
    return pl.pallas_call(
        _ada_kernel,
        grid=(n // tn,),
        in_specs=[pl.BlockSpec((rows, d), lambda j: (0, 0)),
                  pl.BlockSpec((d, tn), lambda j: (0, j)),
                  pl.BlockSpec((1, tn), lambda j: (0, j))],
        out_specs=pl.BlockSpec((rows, tn), lambda j: (0, j)),
        out_shape=jax.ShapeDtypeStruct((rows, n), F32),
        compiler_params=_params("parallel"),
        name="ada",
    )(cc, w, b.reshape(1, n))


def _ffn_kernel(h_ref, sh_ref, sc_ref, gt_ref, gain_ref, wa_ref, wb_ref, wd_ref, o_ref):
    h = h_ref[0]
    u = _modulated(h, gain_ref[...], sh_ref[0], sc_ref[0])
    o_ref[0] = h + (0.5 * gt_ref[0]) * _swiglu(u, wa_ref, wb_ref, wd_ref)


def _ffn(h, shift, scale, gate, gain, wa, wb, wd, *, tm):
    bsz, t, d = h.shape
    f_dim = wa.shape[1]
    per_batch = shift.shape[0] == bsz
    mod_spec = pl.BlockSpec((1, 1, d), (lambda b, i: (b, 0, 0)) if per_batch else (lambda b, i: (0, 0, 0)))
    tok_spec = pl.BlockSpec((1, tm, d), lambda b, i: (b, i, 0))
    return pl.pallas_call(
        _ffn_kernel,
        grid=(bsz, t // tm),
        in_specs=[tok_spec, mod_spec, mod_spec, mod_spec, _const_spec((1, d)),
                  _const_spec((d, f_dim)), _const_spec((d, f_dim)), _const_spec((f_dim, d))],
        out_specs=tok_spec,
        out_shape=jax.ShapeDtypeStruct((bsz, t, d), F32),
        compiler_params=_params("parallel", "parallel"),
        name="ffn",
    )(h, shift, scale, gate, gain.reshape(1, d), wa, wb, wd)


def _chunk_cumsum(g, reverse):
    rows, n = g.shape
    x = g.reshape(rows // 8, 8, n)
    sub = lax.broadcasted_iota(jnp.int32, x.shape, 1)
    for s in (1, 2, 4):
        if reverse:
            x = x + jnp.where(sub < 8 - s, pltpu.roll(x, 8 - s, axis=1), 0.0)
        else:
            x = x + jnp.where(sub >= s, pltpu.roll(x, s, axis=1), 0.0)
    per = CHUNK // 8
    out = []
    for c in range(rows // CHUNK):
        blk = [None] * per
        run = None
        for j in (reversed(range(per)) if reverse else range(per)):
            grp = x[c * per + j]
            blk[j] = grp if run is None else grp + run
            tot = grp[0:1] if reverse else grp[7:8]
            run = tot if run is None else run + tot
        out.extend(blk)
    return jnp.concatenate(out, axis=0)


def _proj_kernel(h_ref, sh_ref, sc_ref, gain_ref, lb_ref, w_ref, *refs, state_only):
    if state_only:
        gf_ref, gb_ref, kf_ref, kb_ref, v_ref, p_ref = refs
    else:
        q_ref, gf_ref, gb_ref, kf_ref, kb_ref, v_ref, go_ref, glu_ref, p_ref = refs
    z0 = 0 if state_only else 1
    h = h_ref[0]
    d = h.shape[-1]
    u = _modulated(h, gain_ref[...], sh_ref[0], sc_ref[0])
    p_ref[...] = _dot(u, w_ref[...])

    def col(g):
        return p_ref[:, g * d:(g + 1) * d]

    if not state_only:
        q = col(0)
        q_ref[0] = (q * jax.nn.sigmoid(q)).astype(BF16)
    for direction, (g_ref, k_ref) in enumerate(((gf_ref, kf_ref), (gb_ref, kb_ref))):
        lb = lb_ref[direction:direction + 1, :]
        f = lb + (1.0 - lb) * jax.nn.sigmoid(col(z0 + direction))
        k_ref[0] = (1.0 - f).astype(BF16)
        g_ref[0] = _chunk_cumsum(jnp.log(f), reverse=direction == 1)
    v_ref[0] = col(z0 + 2).astype(BF16)
    if not state_only:
        go = col(4)
        go_ref[0] = (go * jax.nn.sigmoid(go)).astype(BF16)
        glu_ref[0] = col(5) * jax.nn.sigmoid(col(6))


def _proj(h, shift, scale, gain, lb, w_in, *, tm, state_only=False):
    bsz, t, d = h.shape
    if state_only:
        w_in = w_in[:, d:4 * d]
    per_batch = shift.shape[0] == bsz
    mod_spec = pl.BlockSpec((1, 1, d), (lambda b, i: (b, 0, 0)) if per_batch else (lambda b, i: (0, 0, 0)))
    tok_spec = pl.BlockSpec((1, tm, d), lambda b, i: (b, i, 0))
    state_dtypes = [F32, F32, BF16, BF16, BF16]
    dtypes = state_dtypes if state_only else [BF16] + state_dtypes + [BF16, F32]
    return pl.pallas_call(
        functools.partial(_proj_kernel, state_only=state_only),
        grid=(bsz, t // tm),
        in_specs=[tok_spec, mod_spec, mod_spec, _const_spec((1, d)), _const_spec((2, d)), _const_spec(w_in.shape)],
        out_specs=[tok_spec] * len(dtypes),
        out_shape=[jax.ShapeDtypeStruct((bsz, t, d), dt) for dt in dtypes],
        scratch_shapes=[pltpu.VMEM((tm, w_in.shape[1]), F32)],
        compiler_params=_params("parallel", "parallel"),
        name="proj",
    )(h, shift, scale, gain.reshape(1, d), lb, w_in)


def _scan_local(direction, c, q_ref, g_ref, k_ref, v_ref, o_ref, qt_ref, u_ref, keep):
    fwd = direction == 0
    r0 = pl.multiple_of(c * CHUNK, CHUNK)
    rows = pl.ds(r0, CHUNK)
    G = g_ref[0, rows, :]
    k = k_ref[0, rows, :].astype(F32)
    v = v_ref[0, rows, :]
    gend = G[CHUNK - 1:CHUNK] if fwd else G[0:1]
    kt = (k * jnp.exp(gend - G)).astype(BF16)
    heads = [slice(h * HEAD_DIM, (h + 1) * HEAD_DIM) for h in range(HEADS)]
    if q_ref is None:
        for h, hl in enumerate(heads):
            u_ref[direction, c, h] = lax.dot_general(v[:, hl], kt[:, hl], TN_DIMS, preferred_element_type=F32)
        return

    q = q_ref[0, rows, :].astype(F32)
    qt_ref[direction, rows, :] = (q * jnp.exp(G)).astype(BF16)
    n_sub = CHUNK // SUB
    qis, kis = [], []
    for i in range(n_sub):
        rs = slice(SUB * i, SUB * (i + 1))
        lo, hi = (0, SUB * (i + 1)) if fwd else (SUB * i, CHUNK)
        mid = SUB * i + (SUB // 2 - 1 if fwd else SUB // 2)
        rm = G[mid:mid + 1]
        qis.append((q[rs] * jnp.exp(G[rs] - rm)).astype(BF16))
        parts = [k[lo:hi] * jnp.exp(rm - G[lo:hi])]
        if lo > 0:
            parts.insert(0, jnp.zeros((lo, k.shape[1]), F32))
        if hi < CHUNK:
            parts.append(jnp.zeros((CHUNK - hi, k.shape[1]), F32))
        kis.append(jnp.concatenate(parts, axis=0).astype(BF16))

    a = [[lax.dot_general(qis[i][:, hl], kis[i][:, hl], NT_DIMS, preferred_element_type=F32)
          for i in range(n_sub)] for hl in heads]
    for h, hl in enumerate(heads):
        u_ref[direction, c, h] = lax.dot_general(v[:, hl], kt[:, hl], TN_DIMS, preferred_element_type=F32)
    am = [jnp.where(keep, jnp.concatenate(a[h], axis=0), 0.0).astype(BF16) for h in range(HEADS)]
    for h, hl in enumerate(heads):
        o_ref[0, rows, hl] = _dot(am[h], v[:, hl])


def _scan_carry(direction, c, n_chunks, g_ref, o_ref, qt_ref, u_ref, st_ref):
    fwd = direction == 0
    cc = c if fwd else n_chunks - 1 - c
    r0 = cc * CHUNK
    rows = pl.ds(r0, CHUNK)
    dec = jnp.exp(g_ref[0, pl.ds(r0 + (CHUNK - 1 if fwd else 0), 1), :])
    for h in range(HEADS):
        hl = slice(h * HEAD_DIM, (h + 1) * HEAD_DIM)
        st = st_ref[direction, h]
        if o_ref is not None:
            o_ref[0, rows, hl] += lax.dot_general(qt_ref[direction, rows, hl], st.astype(BF16), NT_DIMS,
                                                  preferred_element_type=F32)
        st_ref[direction, h] = st * dec[:, hl] + u_ref[direction, cc, h]


def _scan_kernel(*refs, n_chunks, readout):
    if readout:
        qf_ref, gf_ref, kf_ref, vf_ref, qb_ref, gb_ref, kb_ref, vb_ref, s0_ref, of_ref, ob_ref, st_ref, qt_ref, u_ref = refs
        sfin_ref = None
    else:
        gf_ref, kf_ref, vf_ref, gb_ref, kb_ref, vb_ref, s0_ref, sfin_ref, st_ref, u_ref = refs
        qf_ref = qb_ref = of_ref = ob_ref = qt_ref = None
    j = pl.program_id(1)

    @pl.when(j == 0)
    def _():
        st_ref[...] = s0_ref[0]

    ri = lax.broadcasted_iota(jnp.int32, (CHUNK, CHUNK), 0)
    ci = lax.broadcasted_iota(jnp.int32, (CHUNK, CHUNK), 1)
    keep_f, keep_b = ci <= ri, ci >= ri

    def local(c, carry):
        _scan_local(0, c, qf_ref, gf_ref, kf_ref, vf_ref, of_ref, qt_ref, u_ref, keep_f)
        _scan_local(1, c, qb_ref, gb_ref, kb_ref, vb_ref, ob_ref, qt_ref, u_ref, keep_b)
        return carry

    lax.fori_loop(0, n_chunks, local, 0, unroll=True)

    for c in range(n_chunks):
        _scan_carry(0, c, n_chunks, gf_ref, of_ref, qt_ref, u_ref, st_ref)
        _scan_carry(1, c, n_chunks, gb_ref, ob_ref, qt_ref, u_ref, st_ref)

    if not readout:
        @pl.when(j == pl.num_programs(1) - 1)
        def _():
            sfin_ref[0] = st_ref[...]


def _scan(q, gf, gb, kf, kb, v, s0, *, tb):
    readout = q is not None
    bsz, t, d = v.shape
    nb = t // tb
    n_chunks = tb // CHUNK
    fwd_spec = pl.BlockSpec((1, tb, d), lambda b, j: (b, j, 0))
    bwd_spec = pl.BlockSpec((1, tb, d), lambda b, j: (b, nb - 1 - j, 0))
    st_shape = (2, HEADS, HEAD_DIM, HEAD_DIM)
    st_spec = pl.BlockSpec((1,) + st_shape, lambda b, j: (b, 0, 0, 0, 0))
    u_scratch = pltpu.VMEM((2, n_chunks, HEADS, HEAD_DIM, HEAD_DIM), F32)
    if readout:
        operands = (q, gf, kf, v, q, gb, kb, v, s0)
        out_specs = [fwd_spec, bwd_spec]
        out_shape = [jax.ShapeDtypeStruct((bsz, t, d), F32)] * 2
        scratch = [pltpu.VMEM(st_shape, F32), pltpu.VMEM((2, tb, d), BF16), u_scratch]
    else:
        operands = (gf, kf, v, gb, kb, v, s0)
        out_specs = st_spec
        out_shape = jax.ShapeDtypeStruct((bsz,) + st_shape, F32)
        scratch = [pltpu.VMEM(st_shape, F32), u_scratch]
    per_dir = (len(operands) - 1) // 2
    return pl.pallas_call(
        functools.partial(_scan_kernel, n_chunks=n_chunks, readout=readout),
        grid=(bsz, nb),
        in_specs=[fwd_spec] * per_dir + [bwd_spec] * per_dir + [st_spec],
        out_specs=out_specs,
        out_shape=out_shape,
        scratch_shapes=scratch,
        compiler_params=_params("parallel", "arbitrary"),
        name="scan",
    )(*operands)


def _conv_branch(xp_ref, r0, n, wdw_ref, bdw_ref, lng_ref, lnb_ref):
    d = xp_ref.shape[1]
    out = []
    for rb in range(r0, r0 + n, CONV_ROWS):
        base = rb + HALO - CONV_PAD
        cols = []
        for c in range(0, d, 128):
            cl = slice(c, c + 128)
            hb = bdw_ref[:, cl]
            for r in range(8):
                offs = [o for o in range(base, base + CONV_WIDTH) if o % 8 == r]
                lo = min(offs)
                shifted = xp_ref[lo:max(offs) + CONV_ROWS, cl]
                acc = None
                for off in offs:
                    term = shifted[off - lo:off - lo + CONV_ROWS] * wdw_ref[off - base:off - base + 1, cl]
                    acc = term if acc is None else acc + term
                hb = hb + acc
            cols.append(hb)
        hb = jnp.concatenate(cols, axis=1)
        mu = jnp.mean(hb, axis=-1, keepdims=True)
        cen = hb - mu
        var = jnp.mean(cen * cen, axis=-1, keepdims=True)
        ln = cen * lax.rsqrt(var + EPS) * lng_ref[...] + lnb_ref[...]
        out.append((ln * jax.nn.sigmoid(ln)).astype(BF16))
    return jnp.concatenate(out, axis=0)


def _mixffn_kernel(h_ref, sh_ref, sc_ref, gt_ref, sh2_ref, sc2_ref, gt2_ref, of_ref, ob_ref, go_ref,
                   glu_ref, prev_ref, next_ref,
                   gain_ref, hn_ref, wdw_ref, bdw_ref, lng_ref, lnb_ref, wpa_ref, wpb_ref, wg_ref, bg_ref, wo_ref,
                   gain2_ref, wa_ref, wb_ref, wd_ref, fg_ref,
                   o_ref, xa_ref, xp_ref):
    i = pl.program_id(1)
    h = h_ref[0]
    tm, d = h.shape

    xp_ref[0:HALO, :] = jnp.where(i > 0, prev_ref[0], 0.0)
    xp_ref[HALO:HALO + tm, :] = glu_ref[0]
    xp_ref[HALO + tm:2 * HALO + tm, :] = jnp.where(i < pl.num_programs(1) - 1, next_ref[0], 0.0)
    act_b = _conv_branch(xp_ref, 0, tm, wdw_ref, bdw_ref, lng_ref, lnb_ref)

    u = _modulated(h, gain_ref[...], sh_ref[0], sc_ref[0])
    gates = jax.nn.sigmoid(_dot(u, wg_ref[...]) + bg_ref[...])

    o = of_ref[0] + ob_ref[0]
    for hd in range(HEADS):
        hl = slice(hd * HEAD_DIM, (hd + 1) * HEAD_DIM)
        oh = _rms(o[:, hl]) * hn_ref[...]
        xa_ref[:, hl] = (oh * go_ref[0, :, hl].astype(F32)).astype(BF16)
    y_a = _dot(xa_ref[...], wpa_ref[...])
    y_b = _dot(act_b, wpb_ref[...])

    mixed = (gates[:, :d] * y_a + gates[:, d:] * y_b).astype(BF16)
    h = h + gt_ref[0] * _dot(mixed, wo_ref[...])

    u2 = _modulated(h, gain2_ref[...], sh2_ref[0], sc2_ref[0])
    h = h + (0.5 * gt2_ref[0]) * _swiglu(u2, wa_ref, wb_ref, wd_ref)
    o_ref[0] = _rms(h) * fg_ref[...]


def _mixffn(h, shift, scale, gate, shift2, scale2, gate2, o_f, o_b, gos, glu, gain, hgrn_norm, w_dw, b_dw,
            ln_g, ln_b, w_pa, w_pb, w_gate, b_gate, w_o, gain2, wa, wb, wd, final_gain, *, tm):
    bsz, t, d = h.shape
    f_dim = wa.shape[1]
    assert tm % CONV_ROWS == 0
    per_tile = tm // HALO
    n_halo = t // HALO
    tok_spec = pl.BlockSpec((1, tm, d), lambda b, i: (b, i, 0))
    mod_spec = pl.BlockSpec((1, 1, d), lambda b, i: (b, 0, 0))
    prev_spec = pl.BlockSpec((1, HALO, d), lambda b, i: (b, jnp.maximum(i * per_tile - 1, 0), 0))
    next_spec = pl.BlockSpec((1, HALO, d), lambda b, i: (b, jnp.minimum((i + 1) * per_tile, n_halo - 1), 0))
    row = lambda a: a.reshape(1, -1)
    return pl.pallas_call(
        _mixffn_kernel,
        grid=(bsz, t // tm),
        in_specs=[tok_spec] + [mod_spec] * 6 + [tok_spec, tok_spec, tok_spec, tok_spec, prev_spec, next_spec,
                  _const_spec((1, d)), _const_spec((1, HEAD_DIM)), _const_spec((CONV_WIDTH, d)),
                  _const_spec((1, d)), _const_spec((1, d)), _const_spec((1, d)),
                  _const_spec((d, d)), _const_spec((d, d)), _const_spec((d, 2 * d)),
                  _const_spec((1, 2 * d)), _const_spec((d, d)),
                  _const_spec((1, d)), _const_spec((d, f_dim)), _const_spec((d, f_dim)), _const_spec((f_dim, d)),
                  _const_spec((1, d))],
        out_specs=tok_spec,
        out_shape=jax.ShapeDtypeStruct((bsz, t, d), F32),
        scratch_shapes=[pltpu.VMEM((tm, d), BF16), pltpu.VMEM((tm + 2 * HALO, d), F32)],
        compiler_params=_params("parallel", "parallel"),
        name="mixffn",
    )(h, shift, scale, gate, shift2, scale2, gate2, o_f, o_b, gos, glu, glu, glu,
      row(gain), row(hgrn_norm), w_dw, row(b_dw), row(ln_g), row(ln_b), w_pa, w_pb, w_gate, row(b_gate), w_o,
      row(gain2), wa, wb, wd, row(final_gain))


def _tile(t, want):
    tm = min(t, want)
    assert t % tm == 0
    return tm


def kernel(x, c, ctx, c_ctx, w_ada, b_ada, norm_ffn1, w_ffn1_up, w_ffn1_down, norm_mix, w_in, lb_logits,
           hgrn_norm, w_dw, b_dw, conv_ln_g, conv_ln_b, w_proj_a, w_proj_b, w_gate, b_gate, w_o, norm_ffn2,
           w_ffn2_up, w_ffn2_down, final_norm):
    bsz, seq, d = x.shape
    depth = w_ada.shape[0]
    assert depth == 1 and d == HEADS * HEAD_DIM
    f_dim = w_ffn1_down.shape[1]
    l = 0

    lb = jnp.cumsum(jax.nn.softmax(lb_logits.astype(F32), axis=0), axis=0)[l]

    n_rows = -(-(bsz + 1) // 8) * 8
    cc = jnp.zeros((n_rows, d), F32).at[:bsz].set(c).at[bsz].set(c_ctx)
    mods = _ada(cc, w_ada[l], b_ada[l])
    m = mods[:bsz].reshape(bsz, N_MOD, 1, d)
    mc = mods[bsz:bsz + 1].reshape(1, N_MOD, 1, d)
    mod = lambda mm, idx: mm[:, idx]

    bf = lambda w: w.astype(BF16)
    w1a, w1b, w1d = bf(w_ffn1_up[l][:, :f_dim]), bf(w_ffn1_up[l][:, f_dim:]), bf(w_ffn1_down[l])
    w2a, w2b, w2d = bf(w_ffn2_up[l][:, :f_dim]), bf(w_ffn2_up[l][:, f_dim:]), bf(w_ffn2_down[l])
    w_in_b = bf(w_in[l])

    tmc = _tile(ctx.shape[1], 256)

    h = _ffn(x, mod(m, 0), mod(m, 1), mod(m, 2), norm_ffn1[l], w1a, w1b, w1d, tm=_tile(seq, 512))
    hc = _ffn(ctx, mod(mc, 0), mod(mc, 1), mod(mc, 2), norm_ffn1[l], w1a, w1b, w1d, tm=tmc)

    gfc, gbc, kfc, kbc, vc = _proj(hc, mod(mc, 3), mod(mc, 4), norm_mix[l], lb, w_in_b, tm=tmc, state_only=True)
    q, gf, gb, kf, kb, v, gos, glu = _proj(h, mod(m, 3), mod(m, 4), norm_mix[l], lb, w_in_b, tm=_tile(seq, 256))

    s0 = jnp.zeros((bsz, 2, HEADS, HEAD_DIM, HEAD_DIM), F32)
    s_ctx = _scan(None, gfc, gbc, kfc, kbc, vc, s0, tb=tmc)
    o_f, o_b = _scan(q, gf, gb, kf, kb, v, s_ctx, tb=_tile(seq, 512))

    return _mixffn(h, mod(m, 3), mod(m, 4), mod(m, 5), mod(m, 6), mod(m, 7), mod(m, 8), o_f, o_b, gos, glu,
                   norm_mix[l], hgrn_norm[l], w_dw[l], b_dw[l], conv_ln_g[l], conv_ln_b[l],
                   bf(w_proj_a[l]), bf(w_proj_b[l]), bf(w_gate[l]), b_gate[l], bf(w_o[l]),
                   norm_ffn2[l], w2a, w2b, w2d, final_norm, tm=_tile(seq, 256))
```

```python
import functools

import jax
import jax.numpy as jnp
from jax import lax
from jax.experimental import pallas as pl
from jax.experimental.pallas import tpu as pltpu

F32 = jnp.float32
BF16 = jnp.bfloat16

EPS = 1e-6
HEADS = 8
HEAD_DIM = 128
N_MOD = 9
CONV_WIDTH = 31
CONV_PAD = CONV_WIDTH // 2
HALO = 16
CONV_ROWS = 64
CHUNK = 64
SUB = 16
V7X_VMEM_LIMIT_BYTES = 56 * 1024 * 1024

NT_DIMS = (((1,), (1,)), ((), ()))
TN_DIMS = (((0,), (0,)), ((), ()))


def _params(*sem):
    return pltpu.CompilerParams(dimension_semantics=sem, vmem_limit_bytes=V7X_VMEM_LIMIT_BYTES)


def _const_spec(shape):
    nd = len(shape)
    return pl.BlockSpec(shape, lambda *_: (0,) * nd, pipeline_mode=pl.Buffered(1))


def _rms(x):
    return x * lax.rsqrt(jnp.mean(x * x, axis=-1, keepdims=True) + EPS)


def _dot(a, b):
    return jnp.dot(a, b, preferred_element_type=F32)


def _modulated(h, gain, shift, scale):
    return ((_rms(h) * gain) * (1.0 + scale) + shift).astype(BF16)


def _swiglu(u, wa_ref, wb_ref, wd_ref):
    a = _dot(u, wa_ref[...])
    b = _dot(u, wb_ref[...])
    return _dot((a * jax.nn.sigmoid(a) * b).astype(BF16), wd_ref[...])


def _ada_kernel(c_ref, w_ref, b_ref, o_ref):
    a = c_ref[...]
    s = (a * jax.nn.sigmoid(a)).astype(BF16)
    o_ref[...] = _dot(s, w_ref[...].astype(BF16)) + b_ref[...]


def _ada(cc, w, b, *, tn=1024):
    rows, d = cc.shape
    n = w.shape[1]
    return pl.pallas_call(
        _ada_kernel,
        grid=(n // tn,),
        in_specs=[pl.BlockSpec((rows, d), lambda j: (0, 0)),
                  pl.BlockSpec((d, tn), lambda j: (0, j)),
                  pl.BlockSpec((1, tn), lambda j: (0, j))],
        out_specs=pl.BlockSpec((rows, tn), lambda j: (0, j)),
        out_shape=jax.ShapeDtypeStruct((rows, n), F32),
        compiler_params=_params("parallel"),
        name="ada",
    )(cc, w, b.reshape(1, n))


def _ffn_kernel(h_ref, sh_ref, sc_ref, gt_ref, gain_ref, wa_ref, wb_ref, wd_ref, o_ref):
    h = h_ref[0]
    u = _modulated(h, gain_ref[...], sh_ref[0], sc_ref[0])
    o_ref[0] = h + (0.5 * gt_ref[0]) * _swiglu(u, wa_ref, wb_ref, wd_ref)


def _ffn(h, shift, scale, gate, gain, wa, wb, wd, *, tm):
    bsz, t, d = h.shape
    f_dim = wa.shape[1]
    per_batch = shift.shape[0] == bsz
    mod_spec = pl.BlockSpec((1, 1, d), (lambda b, i: (b, 0, 0)) if per_batch else (lambda b, i: (0, 0, 0)))
    tok_spec = pl.BlockSpec((1, tm, d), lambda b, i: (b, i, 0))
    return pl.pallas_call(
        _ffn_kernel,
        grid=(bsz, t // tm),
        in_specs=[tok_spec, mod_spec, mod_spec, mod_spec, _const_spec((1, d)),
                  _const_spec((d, f_dim)), _const_spec((d, f_dim)), _const_spec((f_dim, d))],
        out_specs=tok_spec,
        out_shape=jax.ShapeDtypeStruct((bsz, t, d), F32),
        compiler_params=_params("parallel", "parallel"),
        name="ffn",
    )(h, shift, scale, gate, gain.reshape(1, d), wa, wb, wd)


def _chunk_cumsum(g, reverse):
    rows, n = g.shape
    x = g.reshape(rows // 8, 8, n)
    sub = lax.broadcasted_iota(jnp.int32, x.shape, 1)
    for s in (1, 2, 4):
        if reverse:
            x = x + jnp.where(sub < 8 - s, pltpu.roll(x, 8 - s, axis=1), 0.0)
        else:
            x = x + jnp.where(sub >= s, pltpu.roll(x, s, axis=1), 0.0)
    per = CHUNK // 8
    out = []
    for c in range(rows // CHUNK):
        blk = [None] * per
        run = None
        for j in (reversed(range(per)) if reverse else range(per)):
            grp = x[c * per + j]
            blk[j] = grp if run is None else grp + run
            tot = grp[0:1] if reverse else grp[7:8]
            run = tot if run is None else run + tot
        out.extend(blk)
    return jnp.concatenate(out, axis=0)


def _proj_kernel(h_ref, sh_ref, sc_ref, gain_ref, lb_ref, w_ref, *refs, state_only):
    if state_only:
        gf_ref, gb_ref, kf_ref, kb_ref, v_ref = refs
    else:
        q_ref, gf_ref, gb_ref, kf_ref, kb_ref, v_ref, go_ref, glu_ref = refs
    z0 = 0 if state_only else 1
    h = h_ref[0]
    d = h.shape[-1]
    u = _modulated(h, gain_ref[...], sh_ref[0], sc_ref[0])

    def col(g):
        return _dot(u, w_ref[:, g * d:(g + 1) * d])

    if not state_only:
        q = col(0)
        q_ref[0] = (q * jax.nn.sigmoid(q)).astype(BF16)
    for direction, (g_ref, k_ref) in enumerate(((gf_ref, kf_ref), (gb_ref, kb_ref))):
        lb = lb_ref[direction:direction + 1, :]
        f = lb + (1.0 - lb) * jax.nn.sigmoid(col(z0 + direction))
        k_ref[0] = (1.0 - f).astype(BF16)
        g_ref[0] = _chunk_cumsum(jnp.log(f), reverse=direction == 1)
    v_ref[0] = col(z0 + 2).astype(BF16)
    if not state_only:
        go = col(4)
        go_ref[0] = (go * jax.nn.sigmoid(go)).astype(BF16)
        glu_ref[0] = col(5) * jax.nn.sigmoid(col(6))


def _proj(h, shift, scale, gain, lb, w_in, *, tm, state_only=False):
    bsz, t, d = h.shape
    if state_only:
        w_in = w_in[:, d:4 * d]
    per_batch = shift.shape[0] == bsz
    mod_spec = pl.BlockSpec((1, 1, d), (lambda b, i: (b, 0, 0)) if per_batch else (lambda b, i: (0, 0, 0)))
    tok_spec = pl.BlockSpec((1, tm, d), lambda b, i: (b, i, 0))
    state_dtypes = [F32, F32, BF16, BF16, BF16]
    dtypes = state_dtypes if state_only else [BF16] + state_dtypes + [BF16, F32]
    return pl.pallas_call(
        functools.partial(_proj_kernel, state_only=state_only),
        grid=(bsz, t // tm),
        in_specs=[tok_spec, mod_spec, mod_spec, _const_spec((1, d)), _const_spec((2, d)), _const_spec(w_in.shape)],
        out_specs=[tok_spec] * len(dtypes),
        out_shape=[jax.ShapeDtypeStruct((bsz, t, d), dt) for dt in dtypes],
        compiler_params=_params("parallel", "parallel"),
        name="proj",
    )(h, shift, scale, gain.reshape(1, d), lb, w_in)


def _scan_local(direction, c, q_ref, g_ref, k_ref, v_ref, o_ref, qt_ref, u_ref, keep):
    fwd = direction == 0
    r0 = pl.multiple_of(c * CHUNK, CHUNK)
    rows = pl.ds(r0, CHUNK)
    G = g_ref[0, rows, :]
    k = k_ref[0, rows, :].astype(F32)
    v = v_ref[0, rows, :]
    gend = G[CHUNK - 1:CHUNK] if fwd else G[0:1]
    kt = (k * jnp.exp(gend - G)).astype(BF16)
    heads = [slice(h * HEAD_DIM, (h + 1) * HEAD_DIM) for h in range(HEADS)]
    if q_ref is None:
        for h, hl in enumerate(heads):
            u_ref[direction, c, h] = lax.dot_general(v[:, hl], kt[:, hl], TN_DIMS, preferred_element_type=F32)
        return

    q = q_ref[0, rows, :].astype(F32)
    qt_ref[direction, rows, :] = (q * jnp.exp(G)).astype(BF16)
    n_sub = CHUNK // SUB
    qis, kis = [], []
    for i in range(n_sub):
        rs = slice(SUB * i, SUB * (i + 1))
        lo, hi = (0, SUB * (i + 1)) if fwd else (SUB * i, CHUNK)
        mid = SUB * i + (SUB // 2 - 1 if fwd else SUB // 2)
        rm = G[mid:mid + 1]
        qis.append((q[rs] * jnp.exp(G[rs] - rm)).astype(BF16))
        parts = [k[lo:hi] * jnp.exp(rm - G[lo:hi])]
        if lo > 0:
            parts.insert(0, jnp.zeros((lo, k.shape[1]), F32))
        if hi < CHUNK:
            parts.append(jnp.zeros((CHUNK - hi, k.shape[1]), F32))
        kis.append(jnp.concatenate(parts, axis=0).astype(BF16))

    a = [[lax.dot_general(qis[i][:, hl], kis[i][:, hl], NT_DIMS, preferred_element_type=F32)
          for i in range(n_sub)] for hl in heads]
    for h, hl in enumerate(heads):
        u_ref[direction, c, h] = lax.dot_general(v[:, hl], kt[:, hl], TN_DIMS, preferred_element_type=F32)
    am = [jnp.where(keep, jnp.concatenate(a[h], axis=0), 0.0).astype(BF16) for h in range(HEADS)]
    for h, hl in enumerate(heads):
        o_ref[0, rows, hl] = _dot(am[h], v[:, hl])


def _scan_carry(direction, c, n_chunks, g_ref, o_ref, qt_ref, u_ref, st_ref):
    fwd = direction == 0
    cc = c if fwd else n_chunks - 1 - c
    r0 = cc * CHUNK
    rows = pl.ds(r0, CHUNK)
    dec = jnp.exp(g_ref[0, pl.ds(r0 + (CHUNK - 1 if fwd else 0), 1), :])
    for h in range(HEADS):
        hl = slice(h * HEAD_DIM, (h + 1) * HEAD_DIM)
        st = st_ref[direction, h]
        if o_ref is not None:
            o_ref[0, rows, hl] += lax.dot_general(qt_ref[direction, rows, hl], st.astype(BF16), NT_DIMS,
                                                  preferred_element_type=F32)
        st_ref[direction, h] = st * dec[:, hl] + u_ref[direction, cc, h]


def _scan_kernel(*refs, n_chunks, readout):
    if readout:
        qf_ref, gf_ref, kf_ref, vf_ref, qb_ref, gb_ref, kb_ref, vb_ref, s0_ref, of_ref, ob_ref, st_ref, qt_ref, u_ref = refs
        sfin_ref = None
    else:
        gf_ref, kf_ref, vf_ref, gb_ref, kb_ref, vb_ref, s0_ref, sfin_ref, st_ref, u_ref = refs
        qf_ref = qb_ref = of_ref = ob_ref = qt_ref = None
    j = pl.program_id(1)

    @pl.when(j == 0)
    def _():
        st_ref[...] = s0_ref[0]

    ri = lax.broadcasted_iota(jnp.int32, (CHUNK, CHUNK), 0)
    ci = lax.broadcasted_iota(jnp.int32, (CHUNK, CHUNK), 1)
    keep_f, keep_b = ci <= ri, ci >= ri

    def local(c, carry):
        _scan_local(0, c, qf_ref, gf_ref, kf_ref, vf_ref, of_ref, qt_ref, u_ref, keep_f)
        _scan_local(1, c, qb_ref, gb_ref, kb_ref, vb_ref, ob_ref, qt_ref, u_ref, keep_b)
        return carry

    lax.fori_loop(0, n_chunks, local, 0, unroll=True)

    for c in range(n_chunks):
        _scan_carry(0, c, n_chunks, gf_ref, of_ref, qt_ref, u_ref, st_ref)
        _scan_carry(1, c, n_chunks, gb_ref, ob_ref, qt_ref, u_ref, st_ref)

    if not readout:
        @pl.when(j == pl.num_programs(1) - 1)
        def _():
            sfin_ref[0] = st_ref[...]


def _scan(q, gf, gb, kf, kb, v, s0, *, tb):
    readout = q is not None
    bsz, t, d = v.shape
    nb = t // tb
    n_chunks = tb // CHUNK
    fwd_spec = pl.BlockSpec((1, tb, d), lambda b, j: (b, j, 0))
    bwd_spec = pl.BlockSpec((1, tb, d), lambda b, j: (b, nb - 1 - j, 0))
    st_shape = (2, HEADS, HEAD_DIM, HEAD_DIM)
    st_spec = pl.BlockSpec((1,) + st_shape, lambda b, j: (b, 0, 0, 0, 0))
    u_scratch = pltpu.VMEM((2, n_chunks, HEADS, HEAD_DIM, HEAD_DIM), F32)
    if readout:
        operands = (q, gf, kf, v, q, gb, kb, v, s0)
        out_specs = [fwd_spec, bwd_spec]
        out_shape = [jax.ShapeDtypeStruct((bsz, t, d), F32)] * 2
        scratch = [pltpu.VMEM(st_shape, F32), pltpu.VMEM((2, tb, d), BF16), u_scratch]
    else:
        operands = (gf, kf, v, gb, kb, v, s0)
        out_specs = st_spec
        out_shape = jax.ShapeDtypeStruct((bsz,) + st_shape, F32)
        scratch = [pltpu.VMEM(st_shape, F32), u_scratch]
    per_dir = (len(operands) - 1) // 2
    return pl.pallas_call(
        functools.partial(_scan_kernel, n_chunks=n_chunks, readout=readout),
        grid=(bsz, nb),
        in_specs=[fwd_spec] * per_dir + [bwd_spec] * per_dir + [st_spec],
        out_specs=out_specs,
        out_shape=out_shape,
        scratch_shapes=scratch,
        compiler_params=_params("parallel", "arbitrary"),
        name="scan",
    )(*operands)


def _conv_branch(xp_ref, r0, n, wdw_ref, bdw_ref, lng_ref, lnb_ref):
    d = xp_ref.shape[1]
    out = []
    for rb in range(r0, r0 + n, CONV_ROWS):
        base = rb + HALO - CONV_PAD
        cols = []
        for c in range(0, d, 128):
            cl = slice(c, c + 128)
            hb = bdw_ref[:, cl]
            for r in range(8):
                offs = [o for o in range(base, base + CONV_WIDTH) if o % 8 == r]
                lo = min(offs)
                shifted = xp_ref[lo:max(offs) + CONV_ROWS, cl]
                acc = None
                for off in offs:
                    term = shifted[off - lo:off - lo + CONV_ROWS] * wdw_ref[off - base:off - base + 1, cl]
                    acc = term if acc is None else acc + term
                hb = hb + acc
            cols.append(hb)
        hb = jnp.concatenate(cols, axis=1)
        mu = jnp.mean(hb, axis=-1, keepdims=True)
        cen = hb - mu
        var = jnp.mean(cen * cen, axis=-1, keepdims=True)
        ln = cen * lax.rsqrt(var + EPS) * lng_ref[...] + lnb_ref[...]
        out.append((ln * jax.nn.sigmoid(ln)).astype(BF16))
    return jnp.concatenate(out, axis=0)


def _mixffn_kernel(h_ref, sh_ref, sc_ref, gt_ref, sh2_ref, sc2_ref, gt2_ref, of_ref, ob_ref, go_ref,
                   glu_ref, prev_ref, next_ref,
                   gain_ref, hn_ref, wdw_ref, bdw_ref, lng_ref, lnb_ref, wpa_ref, wpb_ref, wg_ref, bg_ref, wo_ref,
                   gain2_ref, wa_ref, wb_ref, wd_ref, fg_ref,
                   o_ref, xa_ref, xp_ref):
    i = pl.program_id(1)
    h = h_ref[0]
    tm, d = h.shape

    xp_ref[0:HALO, :] = jnp.where(i > 0, prev_ref[0], 0.0)
    xp_ref[HALO:HALO + tm, :] = glu_ref[0]
    xp_ref[HALO + tm:2 * HALO + tm, :] = jnp.where(i < pl.num_programs(1) - 1, next_ref[0], 0.0)
    act_b = _conv_branch(xp_ref, 0, tm, wdw_ref, bdw_ref, lng_ref, lnb_ref)

    u = _modulated(h, gain_ref[...], sh_ref[0], sc_ref[0])
    gates = jax.nn.sigmoid(_dot(u, wg_ref[...]) + bg_ref[...])

    o = of_ref[0] + ob_ref[0]
    for hd in range(HEADS):
        hl = slice(hd * HEAD_DIM, (hd + 1) * HEAD_DIM)
        oh = _rms(o[:, hl]) * hn_ref[...]
        xa_ref[:, hl] = (oh * go_ref[0, :, hl].astype(F32)).astype(BF16)
    y_a = _dot(xa_ref[...], wpa_ref[...])
    y_b = _dot(act_b, wpb_ref[...])

    mixed = (gates[:, :d] * y_a + gates[:, d:] * y_b).astype(BF16)
    h = h + gt_ref[0] * _dot(mixed, wo_ref[...])

    u2 = _modulated(h, gain2_ref[...], sh2_ref[0], sc2_ref[0])
    h = h + (0.5 * gt2_ref[0]) * _swiglu(u2, wa_ref, wb_ref, wd_ref)
    o_ref[0] = _rms(h) * fg_ref[...]


def _mixffn(h, shift, scale, gate, shift2, scale2, gate2, o_f, o_b, gos, glu, gain, hgrn_norm, w_dw, b_dw,
            ln_g, ln_b, w_pa, w_pb, w_gate, b_gate, w_o, gain2, wa, wb, wd, final_gain, *, tm):
    bsz, t, d = h.shape
    f_dim = wa.shape[1]
    assert tm % CONV_ROWS == 0
    per_tile = tm // HALO
    n_halo = t // HALO
    tok_spec = pl.BlockSpec((1, tm, d), lambda b, i: (b, i, 0))
    mod_spec = pl.BlockSpec((1, 1, d), lambda b, i: (b, 0, 0))
    prev_spec = pl.BlockSpec((1, HALO, d), lambda b, i: (b, jnp.maximum(i * per_tile - 1, 0), 0))
    next_spec = pl.BlockSpec((1, HALO, d), lambda b, i: (b, jnp.minimum((i + 1) * per_tile, n_halo - 1), 0))
    row = lambda a: a.reshape(1, -1)
    return pl.pallas_call(
        _mixffn_kernel,
        grid=(bsz, t // tm),
        in_specs=[tok_spec] + [mod_spec] * 6 + [tok_spec, tok_spec, tok_spec, tok_spec, prev_spec, next_spec,
                  _const_spec((1, d)), _const_spec((1, HEAD_DIM)), _const_spec((CONV_WIDTH, d)),
                  _const_spec((1, d)), _const_spec((1, d)), _const_spec((1, d)),
                  _const_spec((d, d)), _const_spec((d, d)), _const_spec((d, 2 * d)),
                  _const_spec((1, 2 * d)), _const_spec((d, d)),
                  _const_spec((1, d)), _const_spec((d, f_dim)), _const_spec((d, f_dim)), _const_spec((f_dim, d)),
                  _const_spec((1, d))],
        out_specs=tok_spec,
        out_shape=jax.ShapeDtypeStruct((bsz, t, d), F32),
        scratch_shapes=[pltpu.VMEM((tm, d), BF16), pltpu.VMEM((tm + 2 * HALO, d), F32)],
        compiler_params=_params("parallel", "parallel"),
        name="mixffn",
    )(h, shift, scale, gate, shift2, scale2, gate2, o_f, o_b, gos, glu, glu, glu,
      row(gain), row(hgrn_norm), w_dw, row(b_dw), row(ln_g), row(ln_b), w_pa, w_pb, w_gate, row(b_gate), w_o,
      row(gain2), wa, wb, wd, row(final_gain))


def _tile(t, want):
    tm = min(t, want)
    assert t % tm == 0
    return tm


def kernel(x, c, ctx, c_ctx, w_ada, b_ada, norm_ffn1, w_ffn1_up, w_ffn1_down, norm_mix, w_in, lb_logits,
           hgrn_norm, w_dw, b_dw, conv_ln_g, conv_ln_b, w_proj_a, w_proj_b, w_gate, b_gate, w_o, norm_ffn2,
           w_ffn2_up, w_ffn2_down, final_norm):
    bsz, seq, d = x.shape
    depth = w_ada.shape[0]
    assert depth == 1 and d == HEADS * HEAD_DIM
    f_dim = w_ffn1_down.shape[1]
    l = 0

    lb = jnp.cumsum(jax.nn.softmax(lb_logits.astype(F32), axis=0), axis=0)[l]

    n_rows = -(-(bsz + 1) // 8) * 8
    cc = jnp.zeros((n_rows, d), F32).at[:bsz].set(c).at[bsz].set(c_ctx)
    mods = _ada(cc, w_ada[l], b_ada[l])
    m = mods[:bsz].reshape(bsz, N_MOD, 1, d)
    mc = mods[bsz:bsz + 1].reshape(1, N_MOD, 1, d)
    mod = lambda mm, idx: mm[:, idx]

    bf = lambda w: w.astype(BF16)
    w1a, w1b, w1d = bf(w_ffn1_up[l][:, :f_dim]), bf(w_ffn1_up[l][:, f_dim:]), bf(w_ffn1_down[l])
    w2a, w2b, w2d = bf(w_ffn2_up[l][:, :f_dim]), bf(w_ffn2_up[l][:, f_dim:]), bf(w_ffn2_down[l])
    w_in_b = bf(w_in[l])

    tmc = _tile(ctx.shape[1], 256)

    h = _ffn(x, mod(m, 0), mod(m, 1), mod(m, 2), norm_ffn1[l], w1a, w1b, w1d, tm=_tile(seq, 512))
    hc = _ffn(ctx, mod(mc, 0), mod(mc, 1), mod(mc, 2), norm_ffn1[l], w1a, w1b, w1d, tm=tmc)

    gfc, gbc, kfc, kbc, vc = _proj(hc, mod(mc, 3), mod(mc, 4), norm_mix[l], lb, w_in_b, tm=tmc, state_only=True)
    q, gf, gb, kf, kb, v, gos, glu = _proj(h, mod(m, 3), mod(m, 4), norm_mix[l], lb, w_in_b, tm=_tile(seq, 512))

    s0 = jnp.zeros((bsz, 2, HEADS, HEAD_DIM, HEAD_DIM), F32)
    s_ctx = _scan(None, gfc, gbc, kfc, kbc, vc, s0, tb=tmc)
    o_f, o_b = _scan(q, gf, gb, kf, kb, v, s_ctx, tb=_tile(seq, 512))

    return _mixffn(h, mod(m, 3), mod(m, 4), mod(m, 5), mod(m, 6), mod(m, 7), mod(m, 8), o_f, o_b, gos, glu,
                   norm_mix[l], hgrn_norm[l], w_dw[l], b_dw[l], conv_ln_g[l], conv_ln_b[l],
                   bf(w_proj_a[l]), bf(w_proj_b[l]), bf(w_gate[l]), b_gate[l], bf(w_o[l]),
                   norm_ffn2[l], w2a, w2b, w2d, final_norm, tm=_tile(seq, 256))
```

```python
import functools

import jax
import jax.numpy as jnp
from jax import lax
from jax.experimental import pallas as pl
from jax.experimental.pallas import tpu as pltpu

F32 = jnp.float32
BF16 = jnp.bfloat16

EPS = 1e-6
HEADS = 8
HEAD_DIM = 128
N_MOD = 9
CONV_WIDTH = 31
CONV_PAD = CONV_WIDTH // 2
HALO = 16
CONV_ROWS = 64
CHUNK = 64
SUB = 16
V7X_VMEM_LIMIT_BYTES = 56 * 1024 * 1024

NT_DIMS = (((1,), (1,)), ((), ()))
TN_DIMS = (((0,), (0,)), ((), ()))


def _params(*sem):
    return pltpu.CompilerParams(dimension_semantics=sem, vmem_limit_bytes=V7X_VMEM_LIMIT_BYTES)


def _const_spec(shape):
    nd = len(shape)
    return pl.BlockSpec(shape, lambda *_: (0,) * nd, pipeline_mode=pl.Buffered(1))


def _rms(x):
    return x * lax.rsqrt(jnp.mean(x * x, axis=-1, keepdims=True) + EPS)


def _dot(a, b):
    return jnp.dot(a, b, preferred_element_type=F32)


def _modulated(h, gain, shift, scale):
    return ((_rms(h) * gain) * (1.0 + scale) + shift).astype(BF16)


def _swiglu(u, wa_ref, wb_ref, wd_ref):
    a = _dot(u, wa_ref[...])
    b = _dot(u, wb_ref[...])
    return _dot((a * jax.nn.sigmoid(a) * b).astype(BF16), wd_ref[...])


def _ada_kernel(c_ref, w_ref, b_ref, o_ref):
    a = c_ref[...]
    s = (a * jax.nn.sigmoid(a)).astype(BF16)
    o_ref[...] = _dot(s, w_ref[...].astype(BF16)) + b_ref[...]


def _ada(cc, w, b, *, tn=1024):
    rows, d = cc.shape
    n = w.shape[1]
    return pl.pallas_call(
        _ada_kernel,
        grid=(n // tn,),
        in_specs=[pl.BlockSpec((rows, d), lambda j: (0, 0)),
                  pl.BlockSpec((d, tn), lambda j: (0, j)),
                  pl.BlockSpec((1, tn), lambda j: (0, j))],
        out_specs=pl.BlockSpec((rows, tn), lambda j: (0, j)),
        out_shape=jax.ShapeDtypeStruct((rows, n), F32),
        compiler_params=_params("parallel"),
        name="ada",
    )(cc, w, b.reshape(1, n))


def _ffn_kernel(h_ref, sh_ref, sc_ref, gt_ref, gain_ref, wa_ref, wb_ref, wd_ref, o_ref):
    h = h_ref[0]
    u = _modulated(h, gain_ref[...], sh_ref[0], sc_ref[0])
    o_ref[0] = h + (0.5 * gt_ref[0]) * _swiglu(u, wa_ref, wb_ref, wd_ref)


def _ffn(h, shift, scale, gate, gain, wa, wb, wd, *, tm):
    bsz, t, d = h.shape
    f_dim = wa.shape[1]
    per_batch = shift.shape[0] == bsz
    mod_spec = pl.BlockSpec((1, 1, d), (lambda b, i: (b, 0, 0)) if per_batch else (lambda b, i: (0, 0, 0)))
    tok_spec = pl.BlockSpec((1, tm, d), lambda b, i: (b, i, 0))
    return pl.pallas_call(
        _ffn_kernel,
        grid=(bsz, t // tm),
        in_specs=[tok_spec, mod_spec, mod_spec, mod_spec, _const_spec((1, d)),
                  _const_spec((d, f_dim)), _const_spec((d, f_dim)), _const_spec((f_dim, d))],
        out_specs=tok_spec,
        out_shape=jax.ShapeDtypeStruct((bsz, t, d), F32),
        compiler_params=_params("parallel", "parallel"),
        name="ffn",
    )(h, shift, scale, gate, gain.reshape(1, d), wa, wb, wd)


def _chunk_cumsum(g, reverse):
    rows, n = g.shape
    x = g.reshape(rows // 8, 8, n)
    sub = lax.broadcasted_iota(jnp.int32, x.shape, 1)
    for s in (1, 2, 4):
        if reverse:
            x = x + jnp.where(sub < 8 - s, pltpu.roll(x, 8 - s, axis=1), 0.0)
        else:
            x = x + jnp.where(sub >= s, pltpu.roll(x, s, axis=1), 0.0)
    per = CHUNK // 8
    out = []
    for c in range(rows // CHUNK):
        blk = [None] * per
        run = None
        for j in (reversed(range(per)) if reverse else range(per)):
            grp = x[c * per + j]
            blk[j] = grp if run is None else grp + run
            tot = grp[0:1] if reverse else grp[7:8]
            run = tot if run is None else run + tot
        out.extend(blk)
    return jnp.concatenate(out, axis=0)


def _proj_kernel(h_ref, sh_ref, sc_ref, gain_ref, lb_ref, w_ref, *refs, state_only):
    if state_only:
        gf_ref, gb_ref, kf_ref, kb_ref, v_ref = refs
    else:
        q_ref, gf_ref, gb_ref, kf_ref, kb_ref, v_ref, go_ref, glu_ref = refs
    z0 = 0 if state_only else 1
    h = h_ref[0]
    d = h.shape[-1]
    u = _modulated(h, gain_ref[...], sh_ref[0], sc_ref[0])

    def col(g):
        return _dot(u, w_ref[:, g * d:(g + 1) * d])

    if not state_only:
        q = col(0)
        q_ref[0] = (q * jax.nn.sigmoid(q)).astype(BF16)
    for direction, (g_ref, k_ref) in enumerate(((gf_ref, kf_ref), (gb_ref, kb_ref))):
        lb = lb_ref[direction:direction + 1, :]
        f = lb + (1.0 - lb) * jax.nn.sigmoid(col(z0 + direction))
        k_ref[0] = (1.0 - f).astype(BF16)
        g_ref[0] = _chunk_cumsum(jnp.log(f), reverse=direction == 1)
    v_ref[0] = col(z0 + 2).astype(BF16)
    if not state_only:
        go = col(4)
        go_ref[0] = (go * jax.nn.sigmoid(go)).astype(BF16)
        glu_ref[0] = col(5) * jax.nn.sigmoid(col(6))


def _proj(h, shift, scale, gain, lb, w_in, *, tm, state_only=False):
    bsz, t, d = h.shape
    if state_only:
        w_in = w_in[:, d:4 * d]
    per_batch = shift.shape[0] == bsz
    mod_spec = pl.BlockSpec((1, 1, d), (lambda b, i: (b, 0, 0)) if per_batch else (lambda b, i: (0, 0, 0)))
    tok_spec = pl.BlockSpec((1, tm, d), lambda b, i: (b, i, 0))
    state_dtypes = [F32, F32, BF16, BF16, BF16]
    dtypes = state_dtypes if state_only else [BF16] + state_dtypes + [BF16, F32]
    return pl.pallas_call(
        functools.partial(_proj_kernel, state_only=state_only),
        grid=(bsz, t // tm),
        in_specs=[tok_spec, mod_spec, mod_spec, _const_spec((1, d)), _const_spec((2, d)), _const_spec(w_in.shape)],
        out_specs=[tok_spec] * len(dtypes),
        out_shape=[jax.ShapeDtypeStruct((bsz, t, d), dt) for dt in dtypes],
        compiler_params=_params("parallel", "parallel"),
        name="proj",
    )(h, shift, scale, gain.reshape(1, d), lb, w_in)


def _scan_local(direction, c, q_ref, g_ref, k_ref, v_ref, qt_ref, u_ref):
    fwd = direction == 0
    rows = pl.ds(c * CHUNK, CHUNK)
    G = g_ref[0, rows, :]
    k = k_ref[0, rows, :].astype(F32)
    v = v_ref[0, rows, :]
    gend = G[CHUNK - 1:CHUNK] if fwd else G[0:1]
    kt = (k * jnp.exp(gend - G)).astype(BF16)
    heads = [slice(h * HEAD_DIM, (h + 1) * HEAD_DIM) for h in range(HEADS)]
    if q_ref is None:
        for h, hl in enumerate(heads):
            u_ref[direction, c, h] = lax.dot_general(v[:, hl], kt[:, hl], TN_DIMS, preferred_element_type=F32)
        return None

    q = q_ref[0, rows, :].astype(F32)
    qt_ref[direction, rows, :] = (q * jnp.exp(G)).astype(BF16)
    n_sub = CHUNK // SUB
    qis, kis = [], []
    for i in range(n_sub):
        rs = slice(SUB * i, SUB * (i + 1))
        lo, hi = (0, SUB * (i + 1)) if fwd else (SUB * i, CHUNK)
        mid = SUB * i + (SUB // 2 - 1 if fwd else SUB // 2)
        rm = G[mid:mid + 1]
        qis.append((q[rs] * jnp.exp(G[rs] - rm)).astype(BF16))
        parts = [k[lo:hi] * jnp.exp(rm - G[lo:hi])]
        if lo > 0:
            parts.insert(0, jnp.zeros((lo, k.shape[1]), F32))
        if hi < CHUNK:
            parts.append(jnp.zeros((CHUNK - hi, k.shape[1]), F32))
        kis.append(jnp.concatenate(parts, axis=0).astype(BF16))

    a = [[lax.dot_general(qis[i][:, hl], kis[i][:, hl], NT_DIMS, preferred_element_type=F32)
          for i in range(n_sub)] for hl in heads]
    for h, hl in enumerate(heads):
        u_ref[direction, c, h] = lax.dot_general(kt[:, hl], v[:, hl], TN_DIMS, preferred_element_type=F32)
    return rows, a, v


def _scan_readout(rows, a, v, keep, o_ref):
    for h in range(HEADS):
        hl = slice(h * HEAD_DIM, (h + 1) * HEAD_DIM)
        am = jnp.where(keep, jnp.concatenate(a[h], axis=0), 0.0).astype(BF16)
        o_ref[0, rows, hl] = _dot(am, v[:, hl])


def _scan_carry(direction, n_chunks, g_ref, o_ref, qt_ref, u_ref, st_ref):
    fwd = direction == 0
    order = [c if fwd else n_chunks - 1 - c for c in range(n_chunks)]
    dec = {cc: jnp.exp(g_ref[0, pl.ds(cc * CHUNK + (CHUNK - 1 if fwd else 0), 1), :]) for cc in order}
    for h in range(HEADS):
        hl = slice(h * HEAD_DIM, (h + 1) * HEAD_DIM)
        st = st_ref[direction, h]
        for cc in order:
            if o_ref is None:
                st = st * dec[cc][:, hl] + u_ref[direction, cc, h]
                continue
            rows = pl.ds(cc * CHUNK, CHUNK)
            o_ref[0, rows, hl] += _dot(qt_ref[direction, rows, hl], st.astype(BF16))
            dcol = jnp.broadcast_to(dec[cc][:, hl], (HEAD_DIM, HEAD_DIM)).T
            st = st * dcol + u_ref[direction, cc, h]
        st_ref[direction, h] = st


def _scan_kernel(*refs, n_chunks, readout):
    if readout:
        qf_ref, gf_ref, kf_ref, vf_ref, qb_ref, gb_ref, kb_ref, vb_ref, s0_ref, of_ref, ob_ref, st_ref, qt_ref, u_ref = refs
        sfin_ref = None
    else:
        gf_ref, kf_ref, vf_ref, gb_ref, kb_ref, vb_ref, s0_ref, sfin_ref, st_ref, u_ref = refs
        qf_ref = qb_ref = of_ref = ob_ref = qt_ref = None
    j = pl.program_id(1)

    pairs = [(direction, h) for direction in range(2) for h in range(HEADS)]

    @pl.when(j == 0)
    def _():
        if readout:
            st_ref[...] = s0_ref[0]
        else:
            for direction, h in pairs:
                st_ref[direction, h] = s0_ref[0, direction, h].T

    ri = lax.broadcasted_iota(jnp.int32, (CHUNK, CHUNK), 0)
    ci = lax.broadcasted_iota(jnp.int32, (CHUNK, CHUNK), 1)
    keep_f, keep_b = ci <= ri, ci >= ri

    for c in range(n_chunks):
        first_f = _scan_local(0, c, qf_ref, gf_ref, kf_ref, vf_ref, qt_ref, u_ref)
        first_b = _scan_local(1, c, qb_ref, gb_ref, kb_ref, vb_ref, qt_ref, u_ref)
        if readout:
            _scan_readout(*first_f, keep_f, of_ref)
            _scan_readout(*first_b, keep_b, ob_ref)

    _scan_carry(0, n_chunks, gf_ref, of_ref, qt_ref, u_ref, st_ref)
    _scan_carry(1, n_chunks, gb_ref, ob_ref, qt_ref, u_ref, st_ref)

    if not readout:
        @pl.when(j == pl.num_programs(1) - 1)
        def _():
            for direction, h in pairs:
                sfin_ref[0, direction, h] = st_ref[direction, h].T


def _scan(q, gf, gb, kf, kb, v, s0, *, tb):
    readout = q is not None
    bsz, t, d = v.shape
    nb = t // tb
    n_chunks = tb // CHUNK
    fwd_spec = pl.BlockSpec((1, tb, d), lambda b, j: (b, j, 0))
    bwd_spec = pl.BlockSpec((1, tb, d), lambda b, j: (b, nb - 1 - j, 0))
    st_shape = (2, HEADS, HEAD_DIM, HEAD_DIM)
    st_spec = pl.BlockSpec((1,) + st_shape, lambda b, j: (b, 0, 0, 0, 0))
    u_scratch = pltpu.VMEM((2, n_chunks, HEADS, HEAD_DIM, HEAD_DIM), F32)
    if readout:
        operands = (q, gf, kf, v, q, gb, kb, v, s0)
        out_specs = [fwd_spec, bwd_spec]
        out_shape = [jax.ShapeDtypeStruct((bsz, t, d), F32)] * 2
        scratch = [pltpu.VMEM(st_shape, F32), pltpu.VMEM((2, tb, d), BF16), u_scratch]
    else:
        operands = (gf, kf, v, gb, kb, v, s0)
        out_specs = st_spec
        out_shape = jax.ShapeDtypeStruct((bsz,) + st_shape, F32)
        scratch = [pltpu.VMEM(st_shape, F32), u_scratch]
    per_dir = (len(operands) - 1) // 2
    return pl.pallas_call(
        functools.partial(_scan_kernel, n_chunks=n_chunks, readout=readout),
        grid=(bsz, nb),
        in_specs=[fwd_spec] * per_dir + [bwd_spec] * per_dir + [st_spec],
        out_specs=out_specs,
        out_shape=out_shape,
        scratch_shapes=scratch,
        compiler_params=_params("parallel", "arbitrary"),
        name="scan",
    )(*operands)


def _conv_branch(xp_ref, r0, n, wdw_ref, bdw_ref, lng_ref, lnb_ref):
    d = xp_ref.shape[1]
    out = []
    for rb in range(r0, r0 + n, CONV_ROWS):
        base = rb + HALO - CONV_PAD
        cols = []
        for c in range(0, d, 128):
            cl = slice(c, c + 128)
            hb = bdw_ref[:, cl]
            for r in range(8):
                offs = [o for o in range(base, base + CONV_WIDTH) if o % 8 == r]
                lo = min(offs)
                shifted = xp_ref[lo:max(offs) + CONV_ROWS, cl]
                acc = None
                for off in offs:
                    term = shifted[off - lo:off - lo + CONV_ROWS] * wdw_ref[off - base:off - base + 1, cl]
                    acc = term if acc is None else acc + term
                hb = hb + acc
            cols.append(hb)
        hb = jnp.concatenate(cols, axis=1)
        mu = jnp.mean(hb, axis=-1, keepdims=True)
        cen = hb - mu
        var = jnp.mean(cen * cen, axis=-1, keepdims=True)
        ln = cen * lax.rsqrt(var + EPS) * lng_ref[...] + lnb_ref[...]
        out.append((ln * jax.nn.sigmoid(ln)).astype(BF16))
    return jnp.concatenate(out, axis=0)


def _mixffn_kernel(h_ref, sh_ref, sc_ref, gt_ref, sh2_ref, sc2_ref, gt2_ref, of_ref, ob_ref, go_ref,
                   glu_ref, prev_ref, next_ref,
                   gain_ref, hn_ref, wdw_ref, bdw_ref, lng_ref, lnb_ref, wpa_ref, wpb_ref, wg_ref, bg_ref, wo_ref,
                   gain2_ref, wa_ref, wb_ref, wd_ref, fg_ref,
                   o_ref, xa_ref, xp_ref):
    i = pl.program_id(1)
    h = h_ref[0]
    tm, d = h.shape

    xp_ref[0:HALO, :] = jnp.where(i > 0, prev_ref[0], 0.0)
    xp_ref[HALO:HALO + tm, :] = glu_ref[0]
    xp_ref[HALO + tm:2 * HALO + tm, :] = jnp.where(i < pl.num_programs(1) - 1, next_ref[0], 0.0)
    act_b = _conv_branch(xp_ref, 0, tm, wdw_ref, bdw_ref, lng_ref, lnb_ref)

    u = _modulated(h, gain_ref[...], sh_ref[0], sc_ref[0])
    gates = jax.nn.sigmoid(_dot(u, wg_ref[...]) + bg_ref[...])

    o = of_ref[0] + ob_ref[0]
    for hd in range(HEADS):
        hl = slice(hd * HEAD_DIM, (hd + 1) * HEAD_DIM)
        oh = _rms(o[:, hl]) * hn_ref[...]
        xa_ref[:, hl] = (oh * go_ref[0, :, hl].astype(F32)).astype(BF16)
    y_a = _dot(xa_ref[...], wpa_ref[...])
    y_b = _dot(act_b, wpb_ref[...])

    mixed = (gates[:, :d] * y_a + gates[:, d:] * y_b).astype(BF16)
    h = h + gt_ref[0] * _dot(mixed, wo_ref[...])

    u2 = _modulated(h, gain2_ref[...], sh2_ref[0], sc2_ref[0])
    h = h + (0.5 * gt2_ref[0]) * _swiglu(u2, wa_ref, wb_ref, wd_ref)
    o_ref[0] = _rms(h) * fg_ref[...]


def _mixffn(h, shift, scale, gate, shift2, scale2, gate2, o_f, o_b, gos, glu, gain, hgrn_norm, w_dw, b_dw,
            ln_g, ln_b, w_pa, w_pb, w_gate, b_gate, w_o, gain2, wa, wb, wd, final_gain, *, tm):
    bsz, t, d = h.shape
    f_dim = wa.shape[1]
    assert tm % CONV_ROWS == 0
    per_tile = tm // HALO
    n_halo = t // HALO
    tok_spec = pl.BlockSpec((1, tm, d), lambda b, i: (b, i, 0))
    mod_spec = pl.BlockSpec((1, 1, d), lambda b, i: (b, 0, 0))
    prev_spec = pl.BlockSpec((1, HALO, d), lambda b, i: (b, jnp.maximum(i * per_tile - 1, 0), 0))
    next_spec = pl.BlockSpec((1, HALO, d), lambda b, i: (b, jnp.minimum((i + 1) * per_tile, n_halo - 1), 0))
    row = lambda a: a.reshape(1, -1)
    return pl.pallas_call(
        _mixffn_kernel,
        grid=(bsz, t // tm),
        in_specs=[tok_spec] + [mod_spec] * 6 + [tok_spec, tok_spec, tok_spec, tok_spec, prev_spec, next_spec,
                  _const_spec((1, d)), _const_spec((1, HEAD_DIM)), _const_spec((CONV_WIDTH, d)),
                  _const_spec((1, d)), _const_spec((1, d)), _const_spec((1, d)),
                  _const_spec((d, d)), _const_spec((d, d)), _const_spec((d, 2 * d)),
                  _const_spec((1, 2 * d)), _const_spec((d, d)),
                  _const_spec((1, d)), _const_spec((d, f_dim)), _const_spec((d, f_dim)), _const_spec((f_dim, d)),
                  _const_spec((1, d))],
        out_specs=tok_spec,
        out_shape=jax.ShapeDtypeStruct((bsz, t, d), F32),
        scratch_shapes=[pltpu.VMEM((tm, d), BF16), pltpu.VMEM((tm + 2 * HALO, d), F32)],
        compiler_params=_params("parallel", "parallel"),
        name="mixffn",
    )(h, shift, scale, gate, shift2, scale2, gate2, o_f, o_b, gos, glu, glu, glu,
      row(gain), row(hgrn_norm), w_dw, row(b_dw), row(ln_g), row(ln_b), w_pa, w_pb, w_gate, row(b_gate), w_o,
      row(gain2), wa, wb, wd, row(final_gain))


def _tile(t, want):
    tm = min(t, want)
    assert t % tm == 0
    return tm


def kernel(x, c, ctx, c_ctx, w_ada, b_ada, norm_ffn1, w_ffn1_up, w_ffn1_down, norm_mix, w_in, lb_logits,
           hgrn_norm, w_dw, b_dw, conv_ln_g, conv_ln_b, w_proj_a, w_proj_b, w_gate, b_gate, w_o, norm_ffn2,
           w_ffn2_up, w_ffn2_down, final_norm):
    bsz, seq, d = x.shape
    depth = w_ada.shape[0]
    assert depth == 1 and d == HEADS * HEAD_DIM
    f_dim = w_ffn1_down.shape[1]
    l = 0

    lb = jnp.cumsum(jax.nn.softmax(lb_logits.astype(F32), axis=0), axis=0)[l]

    n_rows = -(-(bsz + 1) // 8) * 8
    cc = jnp.zeros((n_rows, d), F32).at[:bsz].set(c).at[bsz].set(c_ctx)
    mods = _ada(cc, w_ada[l], b_ada[l])
    m = mods[:bsz].reshape(bsz, N_MOD, 1, d)
    mc = mods[bsz:bsz + 1].reshape(1, N_MOD, 1, d)
    mod = lambda mm, idx: mm[:, idx]

    bf = lambda w: w.astype(BF16)
    w1a, w1b, w1d = bf(w_ffn1_up[l][:, :f_dim]), bf(w_ffn1_up[l][:, f_dim:]), bf(w_ffn1_down[l])
    w2a, w2b, w2d = bf(w_ffn2_up[l][:, :f_dim]), bf(w_ffn2_up[l][:, f_dim:]), bf(w_ffn2_down[l])
    w_in_b = bf(w_in[l])

    tmc = _tile(ctx.shape[1], 256)

    h = _ffn(x, mod(m, 0), mod(m, 1), mod(m, 2), norm_ffn1[l], w1a, w1b, w1d, tm=_tile(seq, 512))
    hc = _ffn(ctx, mod(mc, 0), mod(mc, 1), mod(mc, 2), norm_ffn1[l], w1a, w1b, w1d, tm=tmc)

    gfc, gbc, kfc, kbc, vc = _proj(hc, mod(mc, 3), mod(mc, 4), norm_mix[l], lb, w_in_b, tm=tmc, state_only=True)
    q, gf, gb, kf, kb, v, gos, glu = _proj(h, mod(m, 3), mod(m, 4), norm_mix[l], lb, w_in_b, tm=_tile(seq, 512))

    s0 = jnp.zeros((bsz, 2, HEADS, HEAD_DIM, HEAD_DIM), F32)
    s_ctx = _scan(None, gfc, gbc, kfc, kbc, vc, s0, tb=tmc)
    o_f, o_b = _scan(q, gf, gb, kf, kb, v, s_ctx, tb=_tile(seq, 512))

    return _mixffn(h, mod(m, 3), mod(m, 4), mod(m, 5), mod(m, 6), mod(m, 7), mod(m, 8), o_f, o_b, gos, glu,
                   norm_mix[l], hgrn_norm[l], w_dw[l], b_dw[l], conv_ln_g[l], conv_ln_b[l],
                   bf(w_proj_a[l]), bf(w_proj_b[l]), bf(w_gate[l]), b_gate[l], bf(w_o[l]),
                   norm_ffn2[l], w2a, w2b, w2d, final_norm, tm=_tile(seq, 256))
```

```python
import functools

import jax
import jax.numpy as jnp
from jax import lax
from jax.experimental import pallas as pl
from jax.experimental.pallas import tpu as pltpu

F32 = jnp.float32
BF16 = jnp.bfloat16

EPS = 1e-6
HEADS = 8
HEAD_DIM = 128
N_MOD = 9
CONV_WIDTH = 31
CONV_PAD = CONV_WIDTH // 2
HALO = 16
CONV_ROWS = 64
ROW_PARTS = 2
CHUNK = 64
SUB = 16
V7X_VMEM_LIMIT_BYTES = 60 * 1024 * 1024

NT_DIMS = (((1,), (1,)), ((), ()))
TN_DIMS = (((0,), (0,)), ((), ()))


def _params(*sem):
    return pltpu.CompilerParams(dimension_semantics=sem, vmem_limit_bytes=V7X_VMEM_LIMIT_BYTES)


def _const_spec(shape):
    nd = len(shape)
    return pl.BlockSpec(shape, lambda *_: (0,) * nd, pipeline_mode=pl.Buffered(1))


def _rms(x):
    return x * lax.rsqrt(jnp.mean(x * x, axis=-1, keepdims=True) + EPS)


def _dot(a, b):
    return jnp.dot(a, b, preferred_element_type=F32)


def _modulated(h, gain, shift, scale):
    return ((_rms(h) * gain) * (1.0 + scale) + shift).astype(BF16)


def _swiglu(u, wa_ref, wb_ref, wd_ref):
    a = _dot(u, wa_ref[...])
    b = _dot(u, wb_ref[...])
    return _dot((a * jax.nn.sigmoid(a) * b).astype(BF16), wd_ref[...])


def _ada_kernel(c_ref, w_ref, b_ref, o_ref):
    a = c_ref[...]
    s = (a * jax.nn.sigmoid(a)).astype(BF16)
    o_ref[...] = _dot(s, w_ref[...].astype(BF16)) + b_ref[...]


def _ada(cc, w, b, *, tn=1024):
    rows, d = cc.shape
    n = w.shape[1]
    return pl.pallas_call(
        _ada_kernel,
        grid=(n // tn,),
        in_specs=[pl.BlockSpec((rows, d), lambda j: (0, 0)),
                  pl.BlockSpec((d, tn), lambda j: (0, j)),
                  pl.BlockSpec((1, tn), lambda j: (0, j))],
        out_specs=pl.BlockSpec((rows, tn), lambda j: (0, j)),
        out_shape=jax.ShapeDtypeStruct((rows, n), F32),
        compiler_params=_params("parallel"),
        name="ada",
    )(cc, w, b.reshape(1, n))


def _ffn_kernel(h_ref, sh_ref, sc_ref, gt_ref, gain_ref, wa_ref, wb_ref, wd_ref, o_ref):
    h = h_ref[0]
    u = _modulated(h, gain_ref[...], sh_ref[0], sc_ref[0])
    o_ref[0] = h + (0.5 * gt_ref[0]) * _swiglu(u, wa_ref, wb_ref, wd_ref)


def _ffn(h, shift, scale, gate, gain, wa, wb, wd, *, tm):
    bsz, t, d = h.shape
    f_dim = wa.shape[1]
    per_batch = shift.shape[0] == bsz
    mod_spec = pl.BlockSpec((1, 1, d), (lambda b, i: (b, 0, 0)) if per_batch else (lambda b, i: (0, 0, 0)))
    tok_spec = pl.BlockSpec((1, tm, d), lambda b, i: (b, i, 0))
    return pl.pallas_call(
        _ffn_kernel,
        grid=(bsz, t // tm),
        in_specs=[tok_spec, mod_spec, mod_spec, mod_spec, _const_spec((1, d)),
                  _const_spec((d, f_dim)), _const_spec((d, f_dim)), _const_spec((f_dim, d))],
        out_specs=tok_spec,
        out_shape=jax.ShapeDtypeStruct((bsz, t, d), F32),
        compiler_params=_params("parallel", "parallel"),
        name="ffn",
    )(h, shift, scale, gate, gain.reshape(1, d), wa, wb, wd)


def _chunk_cumsum(g, reverse):
    rows, n = g.shape
    x = g.reshape(rows // 8, 8, n)
    sub = lax.broadcasted_iota(jnp.int32, x.shape, 1)
    for s in (1, 2, 4):
        if reverse:
            x = x + jnp.where(sub < 8 - s, pltpu.roll(x, 8 - s, axis=1), 0.0)
        else:
            x = x + jnp.where(sub >= s, pltpu.roll(x, s, axis=1), 0.0)
    per = CHUNK // 8
    out = []
    for c in range(rows // CHUNK):
        blk = [None] * per
        run = None
        for j in (reversed(range(per)) if reverse else range(per)):
            grp = x[c * per + j]
            blk[j] = grp if run is None else grp + run
            tot = grp[0:1] if reverse else grp[7:8]
            run = tot if run is None else run + tot
        out.extend(blk)
    return jnp.concatenate(out, axis=0)


def _proj_kernel(h_ref, sh_ref, sc_ref, gain_ref, lb_ref, w_ref, *refs, state_only):
    if state_only:
        gf_ref, gb_ref, kf_ref, kb_ref, v_ref = refs
    else:
        q_ref, gf_ref, gb_ref, kf_ref, kb_ref, v_ref, go_ref, glu_ref = refs
    z0 = 0 if state_only else 1
    h = h_ref[0]
    d = h.shape[-1]
    u = _modulated(h, gain_ref[...], sh_ref[0], sc_ref[0])

    def col(g):
        return _dot(u, w_ref[:, g * d:(g + 1) * d])

    if not state_only:
        q = col(0)
        q_ref[0] = (q * jax.nn.sigmoid(q)).astype(BF16)
    for direction, (g_ref, k_ref) in enumerate(((gf_ref, kf_ref), (gb_ref, kb_ref))):
        lb = lb_ref[direction:direction + 1, :]
        f = lb + (1.0 - lb) * jax.nn.sigmoid(col(z0 + direction))
        k_ref[0] = (1.0 - f).astype(BF16)
        g_ref[0] = _chunk_cumsum(jnp.log(f), reverse=direction == 1)
    v_ref[0] = col(z0 + 2).astype(BF16)
    if not state_only:
        go = col(4)
        go_ref[0] = (go * jax.nn.sigmoid(go)).astype(BF16)
        glu_ref[0] = col(5) * jax.nn.sigmoid(col(6))


def _proj(h, shift, scale, gain, lb, w_in, *, tm, state_only=False):
    bsz, t, d = h.shape
    if state_only:
        w_in = w_in[:, d:4 * d]
    per_batch = shift.shape[0] == bsz
    mod_spec = pl.BlockSpec((1, 1, d), (lambda b, i: (b, 0, 0)) if per_batch else (lambda b, i: (0, 0, 0)))
    tok_spec = pl.BlockSpec((1, tm, d), lambda b, i: (b, i, 0))
    state_dtypes = [F32, F32, BF16, BF16, BF16]
    dtypes = state_dtypes if state_only else [BF16] + state_dtypes + [BF16, F32]
    return pl.pallas_call(
        functools.partial(_proj_kernel, state_only=state_only),
        grid=(bsz, t // tm),
        in_specs=[tok_spec, mod_spec, mod_spec, _const_spec((1, d)), _const_spec((2, d)), _const_spec(w_in.shape)],
        out_specs=[tok_spec] * len(dtypes),
        out_shape=[jax.ShapeDtypeStruct((bsz, t, d), dt) for dt in dtypes],
        compiler_params=_params("parallel", "parallel"),
        name="proj",
    )(h, shift, scale, gain.reshape(1, d), lb, w_in)


def _scan_local(direction, c, q_ref, g_ref, k_ref, v_ref, qt_ref, u_ref):
    fwd = direction == 0
    rows = pl.ds(c * CHUNK, CHUNK)
    G = g_ref[0, rows, :]
    k = k_ref[0, rows, :].astype(F32)
    v = v_ref[0, rows, :]
    gend = G[CHUNK - 1:CHUNK] if fwd else G[0:1]
    kt = (k * jnp.exp(gend - G)).astype(BF16)
    heads = [slice(h * HEAD_DIM, (h + 1) * HEAD_DIM) for h in range(HEADS)]
    if q_ref is None:
        for h, hl in enumerate(heads):
            u_ref[direction, c, h] = lax.dot_general(v[:, hl], kt[:, hl], TN_DIMS, preferred_element_type=F32)
        return None

    q = q_ref[0, rows, :].astype(F32)
    qt_ref[direction, rows, :] = (q * jnp.exp(G)).astype(BF16)
    n_sub = CHUNK // SUB
    qis, kis = [], []
    for i in range(n_sub):
        rs = slice(SUB * i, SUB * (i + 1))
        lo, hi = (0, SUB * (i + 1)) if fwd else (SUB * i, CHUNK)
        mid = SUB * i + (SUB // 2 - 1 if fwd else SUB // 2)
        rm = G[mid:mid + 1]
        qis.append((q[rs] * jnp.exp(G[rs] - rm)).astype(BF16))
        parts = [k[lo:hi] * jnp.exp(rm - G[lo:hi])]
        if lo > 0:
            parts.insert(0, jnp.zeros((lo, k.shape[1]), F32))
        if hi < CHUNK:
            parts.append(jnp.zeros((CHUNK - hi, k.shape[1]), F32))
        kis.append(jnp.concatenate(parts, axis=0).astype(BF16))

    a = [[lax.dot_general(qis[i][:, hl], kis[i][:, hl], NT_DIMS, preferred_element_type=F32)
          for i in range(n_sub)] for hl in heads]
    for h, hl in enumerate(heads):
        u_ref[direction, c, h] = lax.dot_general(kt[:, hl], v[:, hl], TN_DIMS, preferred_element_type=F32)
    return rows, a, v


def _scan_readout(rows, a, v, keep, o_ref):
    for h in range(HEADS):
        hl = slice(h * HEAD_DIM, (h + 1) * HEAD_DIM)
        am = jnp.where(keep, jnp.concatenate(a[h], axis=0), 0.0).astype(BF16)
        o_ref[0, rows, hl] = _dot(am, v[:, hl])


def _scan_carry(direction, n_chunks, g_ref, o_ref, qt_ref, u_ref, st_ref):
    fwd = direction == 0
    order = [c if fwd else n_chunks - 1 - c for c in range(n_chunks)]
    dec = {cc: jnp.exp(g_ref[0, pl.ds(cc * CHUNK + (CHUNK - 1 if fwd else 0), 1), :]) for cc in order}
    for h in range(HEADS):
        hl = slice(h * HEAD_DIM, (h + 1) * HEAD_DIM)
        st = st_ref[direction, h]
        for cc in order:
            if o_ref is None:
                st = st * dec[cc][:, hl] + u_ref[direction, cc, h]
                continue
            rows = pl.ds(cc * CHUNK, CHUNK)
            o_ref[0, rows, hl] += _dot(qt_ref[direction, rows, hl], st.astype(BF16))
            dcol = jnp.broadcast_to(dec[cc][:, hl], (HEAD_DIM, HEAD_DIM)).T
            st = st * dcol + u_ref[direction, cc, h]
        st_ref[direction, h] = st


def _scan_kernel(*refs, n_chunks, readout):
    if readout:
        qf_ref, gf_ref, kf_ref, vf_ref, qb_ref, gb_ref, kb_ref, vb_ref, s0_ref, of_ref, ob_ref, st_ref, qt_ref, u_ref = refs
        sfin_ref = None
    else:
        gf_ref, kf_ref, vf_ref, gb_ref, kb_ref, vb_ref, s0_ref, sfin_ref, st_ref, u_ref = refs
        qf_ref = qb_ref = of_ref = ob_ref = qt_ref = None
    j = pl.program_id(1)

    pairs = [(direction, h) for direction in range(2) for h in range(HEADS)]

    @pl.when(j == 0)
    def _():
        if readout:
            st_ref[...] = s0_ref[0]
        else:
            for direction, h in pairs:
                st_ref[direction, h] = s0_ref[0, direction, h].T

    ri = lax.broadcasted_iota(jnp.int32, (CHUNK, CHUNK), 0)
    ci = lax.broadcasted_iota(jnp.int32, (CHUNK, CHUNK), 1)
    keep_f, keep_b = ci <= ri, ci >= ri

    for c in range(n_chunks):
        first_f = _scan_local(0, c, qf_ref, gf_ref, kf_ref, vf_ref, qt_ref, u_ref)
        first_b = _scan_local(1, c, qb_ref, gb_ref, kb_ref, vb_ref, qt_ref, u_ref)
        if readout:
            _scan_readout(*first_f, keep_f, of_ref)
            _scan_readout(*first_b, keep_b, ob_ref)

    _scan_carry(0, n_chunks, gf_ref, of_ref, qt_ref, u_ref, st_ref)
    _scan_carry(1, n_chunks, gb_ref, ob_ref, qt_ref, u_ref, st_ref)

    if not readout:
        @pl.when(j == pl.num_programs(1) - 1)
        def _():
            for direction, h in pairs:
                sfin_ref[0, direction, h] = st_ref[direction, h].T


def _scan(q, gf, gb, kf, kb, v, s0, *, tb):
    readout = q is not None
    bsz, t, d = v.shape
    nb = t // tb
    n_chunks = tb // CHUNK
    fwd_spec = pl.BlockSpec((1, tb, d), lambda b, j: (b, j, 0))
    bwd_spec = pl.BlockSpec((1, tb, d), lambda b, j: (b, nb - 1 - j, 0))
    st_shape = (2, HEADS, HEAD_DIM, HEAD_DIM)
    st_spec = pl.BlockSpec((1,) + st_shape, lambda b, j: (b, 0, 0, 0, 0))
    u_scratch = pltpu.VMEM((2, n_chunks, HEADS, HEAD_DIM, HEAD_DIM), F32)
    if readout:
        operands = (q, gf, kf, v, q, gb, kb, v, s0)
        out_specs = [fwd_spec, bwd_spec]
        out_shape = [jax.ShapeDtypeStruct((bsz, t, d), F32)] * 2
        scratch = [pltpu.VMEM(st_shape, F32), pltpu.VMEM((2, tb, d), BF16), u_scratch]
    else:
        operands = (gf, kf, v, gb, kb, v, s0)
        out_specs = st_spec
        out_shape = jax.ShapeDtypeStruct((bsz,) + st_shape, F32)
        scratch = [pltpu.VMEM(st_shape, F32), u_scratch]
    per_dir = (len(operands) - 1) // 2
    return pl.pallas_call(
        functools.partial(_scan_kernel, n_chunks=n_chunks, readout=readout),
        grid=(bsz, nb),
        in_specs=[fwd_spec] * per_dir + [bwd_spec] * per_dir + [st_spec],
        out_specs=out_specs,
        out_shape=out_shape,
        scratch_shapes=scratch,
        compiler_params=_params("parallel", "arbitrary"),
        name="scan",
    )(*operands)


def _conv_branch(xp_ref, r0, n, wdw_ref, bdw_ref, lng_ref, lnb_ref):
    d = xp_ref.shape[1]
    out = []
    for rb in range(r0, r0 + n, CONV_ROWS):
        base = rb + HALO - CONV_PAD
        cols = []
        for c in range(0, d, 128):
            cl = slice(c, c + 128)
            hb = bdw_ref[:, cl]
            for r in range(8):
                offs = [o for o in range(base, base + CONV_WIDTH) if o % 8 == r]
                lo = min(offs)
                shifted = xp_ref[lo:max(offs) + CONV_ROWS, cl]
                acc = None
                for off in offs:
                    term = shifted[off - lo:off - lo + CONV_ROWS] * wdw_ref[off - base:off - base + 1, cl]
                    acc = term if acc is None else acc + term
                hb = hb + acc
            cols.append(hb)
        hb = jnp.concatenate(cols, axis=1)
        mu = jnp.mean(hb, axis=-1, keepdims=True)
        cen = hb - mu
        var = jnp.mean(cen * cen, axis=-1, keepdims=True)
        ln = cen * lax.rsqrt(var + EPS) * lng_ref[...] + lnb_ref[...]
        out.append((ln * jax.nn.sigmoid(ln)).astype(BF16))
    return jnp.concatenate(out, axis=0)


def _mixffn_kernel(h_ref, sh_ref, sc_ref, gt_ref, sh2_ref, sc2_ref, gt2_ref, of_ref, ob_ref, go_ref,
                   glu_ref, prev_ref, next_ref,
                   gain_ref, hn_ref, wdw_ref, bdw_ref, lng_ref, lnb_ref, wpa_ref, wpb_ref, wg_ref, bg_ref, wo_ref,
                   gain2_ref, wa_ref, wb_ref, wd_ref, fg_ref,
                   o_ref, xa_ref, xp_ref):
    i = pl.program_id(1)
    tm, d = h_ref.shape[1:]

    xp_ref[0:HALO, :] = jnp.where(i > 0, prev_ref[0], 0.0)
    xp_ref[HALO:HALO + tm, :] = glu_ref[0]
    xp_ref[HALO + tm:2 * HALO + tm, :] = jnp.where(i < pl.num_programs(1) - 1, next_ref[0], 0.0)

    part = tm // ROW_PARTS
    for r0 in range(0, tm, part):
        rs = slice(r0, r0 + part)
        h = h_ref[0, rs, :]
        act_b = _conv_branch(xp_ref, r0, part, wdw_ref, bdw_ref, lng_ref, lnb_ref)

        u = _modulated(h, gain_ref[...], sh_ref[0], sc_ref[0])
        gates = jax.nn.sigmoid(_dot(u, wg_ref[...]) + bg_ref[...])

        o = of_ref[0, rs, :] + ob_ref[0, rs, :]
        for hd in range(HEADS):
            hl = slice(hd * HEAD_DIM, (hd + 1) * HEAD_DIM)
            oh = _rms(o[:, hl]) * hn_ref[...]
            xa_ref[rs, hl] = (oh * go_ref[0, rs, hl].astype(F32)).astype(BF16)
        y_a = _dot(xa_ref[rs, :], wpa_ref[...])
        y_b = _dot(act_b, wpb_ref[...])

        mixed = (gates[:, :d] * y_a + gates[:, d:] * y_b).astype(BF16)
        h = h + gt_ref[0] * _dot(mixed, wo_ref[...])

        u2 = _modulated(h, gain2_ref[...], sh2_ref[0], sc2_ref[0])
        h = h + (0.5 * gt2_ref[0]) * _swiglu(u2, wa_ref, wb_ref, wd_ref)
        o_ref[0, rs, :] = _rms(h) * fg_ref[...]


def _mixffn(h, shift, scale, gate, shift2, scale2, gate2, o_f, o_b, gos, glu, gain, hgrn_norm, w_dw, b_dw,
            ln_g, ln_b, w_pa, w_pb, w_gate, b_gate, w_o, gain2, wa, wb, wd, final_gain, *, tm):
    bsz, t, d = h.shape
    f_dim = wa.shape[1]
    assert tm % CONV_ROWS == 0
    per_tile = tm // HALO
    n_halo = t // HALO
    tok_spec = pl.BlockSpec((1, tm, d), lambda b, i: (b, i, 0))
    mod_spec = pl.BlockSpec((1, 1, d), lambda b, i: (b, 0, 0))
    prev_spec = pl.BlockSpec((1, HALO, d), lambda b, i: (b, jnp.maximum(i * per_tile - 1, 0), 0))
    next_spec = pl.BlockSpec((1, HALO, d), lambda b, i: (b, jnp.minimum((i + 1) * per_tile, n_halo - 1), 0))
    row = lambda a: a.reshape(1, -1)
    return pl.pallas_call(
        _mixffn_kernel,
        grid=(bsz, t // tm),
        in_specs=[tok_spec] + [mod_spec] * 6 + [tok_spec, tok_spec, tok_spec, tok_spec, prev_spec, next_spec,
                  _const_spec((1, d)), _const_spec((1, HEAD_DIM)), _const_spec((CONV_WIDTH, d)),
                  _const_spec((1, d)), _const_spec((1, d)), _const_spec((1, d)),
                  _const_spec((d, d)), _const_spec((d, d)), _const_spec((d, 2 * d)),
                  _const_spec((1, 2 * d)), _const_spec((d, d)),
                  _const_spec((1, d)), _const_spec((d, f_dim)), _const_spec((d, f_dim)), _const_spec((f_dim, d)),
                  _const_spec((1, d))],
        out_specs=tok_spec,
        out_shape=jax.ShapeDtypeStruct((bsz, t, d), F32),
        scratch_shapes=[pltpu.VMEM((tm, d), BF16), pltpu.VMEM((tm + 2 * HALO, d), F32)],
        compiler_params=_params("parallel", "parallel"),
        name="mixffn",
    )(h, shift, scale, gate, shift2, scale2, gate2, o_f, o_b, gos, glu, glu, glu,
      row(gain), row(hgrn_norm), w_dw, row(b_dw), row(ln_g), row(ln_b), w_pa, w_pb, w_gate, row(b_gate), w_o,
      row(gain2), wa, wb, wd, row(final_gain))


def _tile(t, want):
    tm = min(t, want)
    assert t % tm == 0
    return tm


def kernel(x, c, ctx, c_ctx, w_ada, b_ada, norm_ffn1, w_ffn1_up, w_ffn1_down, norm_mix, w_in, lb_logits,
           hgrn_norm, w_dw, b_dw, conv_ln_g, conv_ln_b, w_proj_a, w_proj_b, w_gate, b_gate, w_o, norm_ffn2,
           w_ffn2_up, w_ffn2_down, final_norm):
    bsz, seq, d = x.shape
    depth = w_ada.shape[0]
    assert depth == 1 and d == HEADS * HEAD_DIM
    f_dim = w_ffn1_down.shape[1]
    l = 0

    lb = jnp.cumsum(jax.nn.softmax(lb_logits.astype(F32), axis=0), axis=0)[l]

    n_rows = -(-(bsz + 1) // 8) * 8
    cc = jnp.zeros((n_rows, d), F32).at[:bsz].set(c).at[bsz].set(c_ctx)
    mods = _ada(cc, w_ada[l], b_ada[l])
    m = mods[:bsz].reshape(bsz, N_MOD, 1, d)
    mc = mods[bsz:bsz + 1].reshape(1, N_MOD, 1, d)
    mod = lambda mm, idx: mm[:, idx]

    bf = lambda w: w.astype(BF16)
    w1a, w1b, w1d = bf(w_ffn1_up[l][:, :f_dim]), bf(w_ffn1_up[l][:, f_dim:]), bf(w_ffn1_down[l])
    w2a, w2b, w2d = bf(w_ffn2_up[l][:, :f_dim]), bf(w_ffn2_up[l][:, f_dim:]), bf(w_ffn2_down[l])
    w_in_b = bf(w_in[l])

    tmc = _tile(ctx.shape[1], 256)

    h = _ffn(x, mod(m, 0), mod(m, 1), mod(m, 2), norm_ffn1[l], w1a, w1b, w1d, tm=_tile(seq, 512))
    hc = _ffn(ctx, mod(mc, 0), mod(mc, 1), mod(mc, 2), norm_ffn1[l], w1a, w1b, w1d, tm=tmc)

    gfc, gbc, kfc, kbc, vc = _proj(hc, mod(mc, 3), mod(mc, 4), norm_mix[l], lb, w_in_b, tm=tmc, state_only=True)
    q, gf, gb, kf, kb, v, gos, glu = _proj(h, mod(m, 3), mod(m, 4), norm_mix[l], lb, w_in_b, tm=_tile(seq, 512))

    s0 = jnp.zeros((bsz, 2, HEADS, HEAD_DIM, HEAD_DIM), F32)
    s_ctx = _scan(None, gfc, gbc, kfc, kbc, vc, s0, tb=tmc)
    o_f, o_b = _scan(q, gf, gb, kf, kb, v, s_ctx, tb=_tile(seq, 512))

    return _mixffn(h, mod(m, 3), mod(m, 4), mod(m, 5), mod(m, 6), mod(m, 7), mod(m, 8), o_f, o_b, gos, glu,
                   norm_mix[l], hgrn_norm[l], w_dw[l], b_dw[l], conv_ln_g[l], conv_ln_b[l],
                   bf(w_proj_a[l]), bf(w_proj_b[l]), bf(w_gate[l]), b_gate[l], bf(w_o[l]),
                   norm_ffn2[l], w2a, w2b, w2d, final_norm, tm=_tile(seq, 512))
```
